```python
import math
import jax
import jax.numpy as jnp
from jax import lax
import numpy as np

D_MODEL = 1024
BATCH = 8
SEQ = 2048
DEPTH = 2

GRID_W = 64
CTX_LEN = 256
EPS = 1e-6
HY_GROUPS = 4
D_HY = 256
HY_ORDER = 2
HY_BANDS = 16
HY_EMB = 1 + 2 * HY_BANDS
HY_FFN = 64
HY_DECAY_TARGET = 1e-2
HY_FAST_DECAY = 0.3
HY_SLOW_DECAY = 1.5
HY_FILTER_SCALE = 0.07
SG_GROUPS = 4
D_SG = 256
SG_GROUP_DIM = D_SG // SG_GROUPS
SG_CHUNK = 128
ATT_HEADS = 4
QK_DIM = 64
V_DIM = 128
D_ATT = ATT_HEADS * V_DIM
ATT_BLOCK = 128
ROPE_PAIRS = QK_DIM // 4
ROPE_BASE = 10000.0
D_MIX = D_HY + D_SG + D_ATT
IN_A = 3 * D_HY
IN_B = 2 * D_SG
IN_C = 2 * ATT_HEADS * 2 * QK_DIM + D_ATT
D_IN = IN_A + IN_B + IN_C
D_FF = 2816

kernel_name = 'hybrid_hyena_gmlp_diffattn_dit_block'


def rmsnorm(x, g):
    xf = x.astype(jnp.float32)
    y = xf * lax.rsqrt(jnp.mean(xf * xf, axis=-1, keepdims=True) + EPS)
    return (y * g.astype(jnp.float32)).astype(x.dtype)


def layernorm(x, g, b):
    xf = x.astype(jnp.float32)
    mu = jnp.mean(xf, axis=-1, keepdims=True)
    var = jnp.mean(jnp.square(xf - mu), axis=-1, keepdims=True)
    y = (xf - mu) * lax.rsqrt(var + EPS)
    return (y * g.astype(jnp.float32) + b.astype(jnp.float32)).astype(x.dtype)


def modulate(h, shift, scale):
    return h * (1.0 + scale) + shift


def dwconv3(x, w, b):
    L = x.shape[1]
    xp = jnp.pad(x, ((0, 0), (1, 1), (0, 0)))
    return xp[:, :L] * w[0] + xp[:, 1:L + 1] * w[1] + xp[:, 2:] * w[2] + b


def hyena_filters(L, w1, b1, w2, b2, w3, freq):
    f32 = jnp.float32
    t = jnp.linspace(0.0, 1.0, L, dtype=f32)[:, None]
    t_r = jnp.arange(L, dtype=f32)[:, None]
    bands = jnp.linspace(1e-4, HY_BANDS - 1, HY_BANDS, dtype=f32)[None, :]
    w = 2.0 * math.pi * t_r / L
    z = jnp.concatenate([t, jnp.cos(bands * w), -jnp.sin(bands * w)], axis=-1)
    fr = freq.astype(f32)
    h = jnp.sin(fr * (z @ w1.astype(f32) + b1.astype(f32)))
    h = jnp.sin(fr * (h @ w2.astype(f32) + b2.astype(f32)))
    h = (h @ w3.astype(f32)).reshape(L, HY_ORDER, 2, D_HY)
    min_decay = math.log(HY_DECAY_TARGET) / HY_SLOW_DECAY
    max_decay = math.log(HY_DECAY_TARGET) / HY_FAST_DECAY
    deltas = jnp.abs(jnp.linspace(min_decay, max_decay, D_HY, dtype=f32))
    h = h * jnp.exp(-t * deltas)[:, None, None, :]
    hf, hb = h[:, :, 0], h[:, :, 1]
    k = jnp.concatenate([hf, jnp.zeros_like(hf[:1]), hb[:0:-1]], axis=0)
    return jnp.fft.rfft(k, axis=0)


def fftconv(u, kf):
    L = u.shape[1]
    U = jnp.fft.rfft(u, n=2 * L, axis=1)
    return jnp.fft.irfft(U * kf[None], n=2 * L, axis=1)[:, :L]


def hyena_mixer(p, conv_w, conv_b, kf, bias):
    p = dwconv3(p, conv_w, conv_b)
    v, x1, x2 = jnp.split(p, 3, axis=-1)
    bias = bias.astype(jnp.float32)
    z = v.astype(jnp.float32)
    z = x1.astype(jnp.float32) * (fftconv(z, kf[:, 0]) + bias[0] * z)
    z = x2.astype(jnp.float32) * (fftconv(z, kf[:, 1]) + bias[1] * z)
    return z.astype(p.dtype)


def sgu_mixer(p, ln_g, ln_b, w_s, b_s):
    B, L, _ = p.shape
    p = jax.nn.gelu(p)
    u, v = jnp.split(p, 2, axis=-1)
    v = layernorm(v.reshape(B, L, SG_GROUPS, SG_GROUP_DIM), ln_g, ln_b)
    v = v.reshape(B, L // SG_CHUNK, SG_CHUNK, SG_GROUPS, SG_GROUP_DIM)
    s = jnp.einsum('gpq,bcqgd->bcpgd', w_s, v) + b_s.T[None, None, :, :, None]
    return u * s.reshape(B, L, D_SG)


def axial_rope_tables(L):
    rows = L // GRID_W
    row = jnp.broadcast_to(jnp.arange(rows, dtype=jnp.float32)[:, None], (rows, GRID_W)).reshape(-1)
    col = jnp.broadcast_to(jnp.arange(GRID_W, dtype=jnp.float32)[None, :], (rows, GRID_W)).reshape(-1)
    inv = ROPE_BASE ** (-jnp.arange(ROPE_PAIRS, dtype=jnp.float32) / ROPE_PAIRS)
    ang = jnp.stack([row[:, None] * inv, col[:, None] * inv], axis=1)
    return jnp.cos(ang), jnp.sin(ang)


def apply_rope(x, cos, sin):
    B, L, H = x.shape[:3]
    xr = x.reshape(B, L, H, 2, 2, 2, ROPE_PAIRS)
    c = cos.astype(x.dtype)[None, :, None, None, :, :]
    s = sin.astype(x.dtype)[None, :, None, None, :, :]
    x1, x2 = xr[..., 0, :], xr[..., 1, :]
    out = jnp.stack([x1 * c - x2 * s, x2 * c + x1 * s], axis=-2)
    return out.reshape(x.shape)


def diff_qkv(pc, qn, kn):
    B, L, _ = pc.shape
    q, k, v = jnp.split(pc, [2 * ATT_HEADS * QK_DIM, 4 * ATT_HEADS * QK_DIM], axis=-1)
    q = rmsnorm(q.reshape(B, L, ATT_HEADS, 2, QK_DIM), qn)
    k = rmsnorm(k.reshape(B, L, ATT_HEADS, 2, QK_DIM), kn)
    return q, k, v.reshape(B, L, ATT_HEADS, V_DIM)


def diff_attend(q, k, v, lam):
    s = jnp.einsum('bqhcd,bkhcd->bhcqk', q, k, preferred_element_type=jnp.float32) * (QK_DIM ** -0.5)
    p = jax.nn.softmax(s, axis=-1)
    a = p[:, :, 0] - lam * p[:, :, 1]
    return jnp.einsum('bhqk,bkhe->bqhe', a.astype(v.dtype), v)


def diff_attention_blocks(q, k, v, lam):
    B, L, H, _, dq = q.shape
    nb = L // ATT_BLOCK
    qb = jnp.moveaxis(q.reshape(B, nb, ATT_BLOCK, H, 2, dq), 1, 0)
    o = lax.map(lambda qq: diff_attend(qq, k, v, lam), qb)
    return jnp.moveaxis(o, 0, 1).reshape(B, L, H, v.shape[-1])


def diff_out(o, g, lam_init):
    B, L = o.shape[:2]
    return (rmsnorm(o, g) * (1.0 - lam_init)).reshape(B, L, D_ATT)


def conv_ffn(h, w_up, cw, cb, w_down):
    u = dwconv3(h @ w_up, cw, cb)
    g, val = jnp.split(u, 2, axis=-1)
    return (jax.nn.silu(g) * val) @ w_down


def setup_inputs(seed: int = 0) -> dict:
    key = jax.random.key(seed)
    ks = iter(jax.random.split(key, 40))

    def nrm(shape, scale):
        return jax.random.normal(next(ks), shape, jnp.float32) * scale

    def gain(shape):
        return 1.0 + nrm(shape, 0.02)

    return {
        'x': nrm((BATCH, SEQ, D_MODEL), 1.0),
        'c': nrm((BATCH, D_MODEL), 1.0),
        'ctx': nrm((BATCH, CTX_LEN, D_MODEL), 1.0),
        'c_ctx': nrm((D_MODEL,), 1.0),
        'w_mod': nrm((DEPTH, D_MODEL, 6 * D_MODEL), 0.5 * D_MODEL ** -0.5),
        'b_mod': nrm((DEPTH, 6 * D_MODEL), 0.02),
        'norm1_g': gain((DEPTH, D_MODEL)),
        'w_in': nrm((DEPTH, D_MODEL, D_IN), D_MODEL ** -0.5),
        'hy_conv_w': nrm((DEPTH, 3, IN_A), 3 ** -0.5),
        'hy_conv_b': nrm((DEPTH, IN_A), 0.02),
        'hy_w1': nrm((DEPTH, HY_EMB, HY_FFN), HY_EMB ** -0.5),
        'hy_b1': nrm((DEPTH, HY_FFN), 0.02),
        'hy_w2': nrm((DEPTH, HY_FFN, HY_FFN), HY_FFN ** -0.5),
        'hy_b2': nrm((DEPTH, HY_FFN), 0.02),
        'hy_w3': nrm((DEPTH, HY_FFN, HY_ORDER * 2 * D_HY), HY_FILTER_SCALE * HY_FFN ** -0.5),
        'hy_freq': gain((DEPTH, HY_FFN)),
        'hy_bias': nrm((DEPTH, HY_ORDER, D_HY), 0.1),
        'sg_ln_g': gain((DEPTH, SG_GROUPS, SG_GROUP_DIM)),
        'sg_ln_b': nrm((DEPTH, SG_GROUPS, SG_GROUP_DIM), 0.02),
        'sg_w': nrm((DEPTH, SG_GROUPS, SG_CHUNK, SG_CHUNK), SG_CHUNK ** -0.5),
        'sg_b': gain((DEPTH, SG_GROUPS, SG_CHUNK)),
        'q_norm_g': gain((DEPTH, QK_DIM)),
        'k_norm_g': gain((DEPTH, QK_DIM)),
        'lam_q1': nrm((DEPTH, QK_DIM), 0.1),
        'lam_k1': nrm((DEPTH, QK_DIM), 0.1),
        'lam_q2': nrm((DEPTH, QK_DIM), 0.1),
        'lam_k2': nrm((DEPTH, QK_DIM), 0.1),
        'subln_g': gain((DEPTH, V_DIM)),
        'w_out': nrm((DEPTH, D_MIX, D_MODEL), D_MIX ** -0.5),
        'norm2_g': gain((DEPTH, D_MODEL)),
        'ffn_w_up': nrm((DEPTH, D_MODEL, 2 * D_FF), D_MODEL ** -0.5),
        'ffn_conv_w': nrm((DEPTH, 3, 2 * D_FF), 3 ** -0.5),
        'ffn_conv_b': nrm((DEPTH, 2 * D_FF), 0.02),
        'ffn_w_down': nrm((DEPTH, D_FF, D_MODEL), D_FF ** -0.5),
    }


def reference(x, c, ctx, c_ctx, w_mod, b_mod, norm1_g, w_in, hy_conv_w, hy_conv_b, hy_w1, hy_b1,
              hy_w2, hy_b2, hy_w3, hy_freq, hy_bias, sg_ln_g, sg_ln_b, sg_w, sg_b, q_norm_g, k_norm_g,
              lam_q1, lam_k1, lam_q2, lam_k2, subln_g, w_out, norm2_g, ffn_w_up, ffn_conv_w,
              ffn_conv_b, ffn_w_down):
    f32 = jnp.float32
    xs, cs = x, ctx
    L = xs.shape[1]
    Lc = cs.shape[1]
    rope_cos, rope_sin = axial_rope_tables(L)
    for l in range(DEPTH):
        last = l == DEPTH - 1
        lam_init = 0.8 - 0.6 * math.exp(-0.3 * l)
        lam = (jnp.exp(jnp.sum(lam_q1[l].astype(f32) * lam_k1[l].astype(f32)))
               - jnp.exp(jnp.sum(lam_q2[l].astype(f32) * lam_k2[l].astype(f32))) + lam_init)
        m = (jax.nn.silu(c) @ w_mod[l] + b_mod[l])[:, None, :]
        sh1, sc1, g1, sh2, sc2, g2 = jnp.split(m, 6, axis=-1)
        mc = jax.nn.silu(c_ctx) @ w_mod[l] + b_mod[l]
        csh1, csc1, cg1, csh2, csc2, cg2 = jnp.split(mc, 6, axis=-1)

        h = modulate(rmsnorm(xs, norm1_g[l]), sh1, sc1)
        pa, pb, pc = jnp.split(h @ w_in[l], [IN_A, IN_A + IN_B], axis=-1)
        hc = modulate(rmsnorm(cs, norm1_g[l]), csh1, csc1)
        if last:
            pc_c = hc @ w_in[l][:, IN_A + IN_B:]
        else:
            pa_c, pb_c, pc_c = jnp.split(hc @ w_in[l], [IN_A, IN_A + IN_B], axis=-1)

        q, k, v = diff_qkv(pc, q_norm_g[l], k_norm_g[l])
        q = apply_rope(q, rope_cos, rope_sin)
        k = apply_rope(k, rope_cos, rope_sin)
        qc, kc, vc = diff_qkv(pc_c, q_norm_g[l], k_norm_g[l])
        k_all = jnp.concatenate([k, kc], axis=1)
        v_all = jnp.concatenate([v, vc], axis=1)
        o_att = diff_out(diff_attention_blocks(q, k_all, v_all, lam), subln_g[l], lam_init)

        kf = hyena_filters(L, hy_w1[l], hy_b1[l], hy_w2[l], hy_b2[l], hy_w3[l], hy_freq[l])
        o_hy = hyena_mixer(pa, hy_conv_w[l], hy_conv_b[l], kf, hy_bias[l])
        o_sg = sgu_mixer(pb, sg_ln_g[l], sg_ln_b[l], sg_w[l], sg_b[l])

        mix = jnp.concatenate([o_hy, o_sg, o_att], axis=-1) @ w_out[l]
        new_xs = xs + g1 * mix

        if not last:
            oc_att = diff_out(diff_attend(qc, kc, vc, lam), subln_g[l], lam_init)
            kf_c = hyena_filters(Lc, hy_w1[l], hy_b1[l], hy_w2[l], hy_b2[l], hy_w3[l], hy_freq[l])
            oc_hy = hyena_mixer(pa_c, hy_conv_w[l], hy_conv_b[l], kf_c, hy_bias[l])
            oc_sg = sgu_mixer(pb_c, sg_ln_g[l], sg_ln_b[l], sg_w[l], sg_b[l])
            mix_c = jnp.concatenate([oc_hy, oc_sg, oc_att], axis=-1) @ w_out[l]
            new_cs = cs + cg1 * mix_c
            hc2 = modulate(rmsnorm(new_cs, norm2_g[l]), csh2, csc2)
            cs = new_cs + cg2 * conv_ffn(hc2, ffn_w_up[l], ffn_conv_w[l], ffn_conv_b[l], ffn_w_down[l])

        h2 = modulate(rmsnorm(new_xs, norm2_g[l]), sh2, sc2)
        xs = new_xs + g2 * conv_ffn(h2, ffn_w_up[l], ffn_conv_w[l], ffn_conv_b[l], ffn_w_down[l])
    return xs
```

```python
import functools
import math

import jax
import jax.numpy as jnp
import numpy as np
from jax import lax
from jax.experimental import pallas as pl
from jax.experimental.pallas import tpu as pltpu

F32 = jnp.float32
BF16 = jnp.bfloat16
HIGHEST = lax.Precision.HIGHEST

D_MODEL = 1024
DEPTH = 2
GRID_W = 64
EPS = 1e-6
D_HY = 256
HY_ORDER = 2
HY_BANDS = 16
HY_EMB = 1 + 2 * HY_BANDS
HY_FFN = 64
HY_DECAY_TARGET = 1e-2
HY_FAST_DECAY = 0.3
HY_SLOW_DECAY = 1.5
SG_GROUPS = 4
D_SG = 256
SG_CHUNK = 128
ATT_HEADS = 4
QK_DIM = 64
V_DIM = 128
D_ATT = ATT_HEADS * V_DIM
ROPE_PAIRS = QK_DIM // 4
ROPE_BASE = 10000.0
D_MIX = D_HY + D_SG + D_ATT
IN_A = 3 * D_HY
IN_B = 2 * D_SG
D_QK = 2 * ATT_HEADS * QK_DIM
IN_C = 2 * D_QK + D_ATT
D_IN = IN_A + IN_B + IN_C
D_FF = 2816

LANES = 128
BF16_SUBLANES = 16
VMEM_LIMIT_BYTES = 56 * 1024 * 1024

MOD_ROWS = 16
FF_CHUNK = 256
HALO = BF16_SUBLANES


def _params(sem, vmem=VMEM_LIMIT_BYTES):
    return pltpu.CompilerParams(dimension_semantics=sem, vmem_limit_bytes=vmem)


def _resident(shape):
    nd = len(shape)
    return pl.BlockSpec(shape, lambda *_: (0,) * nd, pipeline_mode=pl.Buffered(1))


@functools.lru_cache(maxsize=None)
def _dft_tables(seq):
    n2 = 2 * seq
    f = np.arange(seq, dtype=np.int64)
    m = ((2 * f[:, None] + 1) * (2 * f[None, :] + 1)) % (4 * n2)
    ang = (2.0 * np.pi / (4 * n2)) * m.astype(np.float64)
    half = np.pi * (2 * f + 1) / (2.0 * n2)
    return (np.cos(ang).astype(np.float32), np.sin(ang).astype(np.float32),
            np.cos(half).astype(np.float32)[:, None], np.sin(half).astype(np.float32)[:, None])


@functools.lru_cache(maxsize=None)
def _filter_tables(seq):
    t = np.linspace(0.0, 1.0, seq, dtype=np.float32)[:, None]
    t_r = np.arange(seq, dtype=np.float32)[:, None]
    bands = np.linspace(1e-4, HY_BANDS - 1, HY_BANDS, dtype=np.float32)[None, :]
    w = (2.0 * math.pi * t_r / seq).astype(np.float32)
    z = np.concatenate([t, np.cos(bands * w), -np.sin(bands * w)], axis=-1).astype(np.float32)
    zp = np.zeros((seq, LANES), np.float32)
    zp[:, :HY_EMB] = z
    min_decay = math.log(HY_DECAY_TARGET) / HY_SLOW_DECAY
    max_decay = math.log(HY_DECAY_TARGET) / HY_FAST_DECAY
    deltas = np.abs(np.linspace(min_decay, max_decay, D_HY, dtype=np.float32))
    decay = np.exp(-t * deltas[None, :]).astype(np.float32)
    return zp, decay


@functools.lru_cache(maxsize=None)
def _rope_tables(seq):
    pos = np.arange(seq)
    row = (pos // GRID_W).astype(np.float32)
    col = (pos % GRID_W).astype(np.float32)
    inv = (ROPE_BASE ** (-np.arange(ROPE_PAIRS, dtype=np.float32) / ROPE_PAIRS)).astype(np.float32)
    lane = np.arange(LANES)
    axis = (lane % QK_DIM) // (2 * ROPE_PAIRS)
    half = (lane % (2 * ROPE_PAIRS)) // ROPE_PAIRS
    pair = lane % ROPE_PAIRS
    p = np.where(axis[None, :] == 0, row[:, None], col[:, None]).astype(np.float32)
    ang = (p * inv[pair][None, :]).astype(np.float32)
    cos, sin = np.cos(ang).astype(np.float32), np.sin(ang).astype(np.float32)
    sin_lo = np.where(half[None, :] == 0, -sin, 0.0).astype(np.float32)
    sin_hi = np.where(half[None, :] == 1, sin, 0.0).astype(np.float32)
    return cos, sin_lo, sin_hi


def _group_sum(x):
    lane = lax.broadcasted_iota(jnp.int32, (1, LANES), 1)
    lo = lane < QK_DIM
    s_lo = jnp.sum(jnp.where(lo, x, 0.0), axis=-1, keepdims=True)
    s_hi = jnp.sum(jnp.where(lo, 0.0, x), axis=-1, keepdims=True)
    return jnp.where(lo, s_lo, s_hi)


def _rmsnorm_rows(x, g):
    return x * lax.rsqrt(jnp.mean(x * x, axis=-1, keepdims=True) + EPS) * g


def _bdot(a, b):
    return jnp.dot(a, b, preferred_element_type=F32)


def _mod_kernel(cc_ref, w_ref, b_ref, o_ref):
    cc = cc_ref[...]
    a = cc * jax.nn.sigmoid(cc)
    o_ref[0] = jnp.dot(a, w_ref[0], preferred_element_type=F32, precision=HIGHEST) + b_ref[0]


def _modulation(cc, w_mod, b_mod):
    depth, d, n = w_mod.shape
    tn = 768
    return pl.pallas_call(
        _mod_kernel,
        out_shape=jax.ShapeDtypeStruct((depth, MOD_ROWS, n), F32),
        grid=(depth, n // tn),
        in_specs=[pl.BlockSpec((MOD_ROWS, d), lambda l, j: (0, 0)),
                  pl.BlockSpec((1, d, tn), lambda l, j: (l, 0, j)),
                  pl.BlockSpec((1, 1, tn), lambda l, j: (l, 0, j))],
        out_specs=pl.BlockSpec((1, MOD_ROWS, tn), lambda l, j: (l, 0, j)),
        compiler_params=_params(("parallel", "parallel")),
        name="adaln_modulation",
    )(cc, w_mod, b_mod.reshape(depth, 1, n))


def _filter_kernel(z_ref, w1_ref, b1_ref, fr_ref, w2_ref, b2_ref, w3_ref, dec_ref, cm_ref, sm_ref,
                   ch_ref, sh_ref, o_ref, *, seq):
    fr = fr_ref[0]
    h = jnp.sin(fr * (jnp.dot(z_ref[...], w1_ref[0], preferred_element_type=F32, precision=HIGHEST) + b1_ref[0]))
    h = jnp.sin(fr * (jnp.dot(h, w2_ref[0], preferred_element_type=F32, precision=HIGHEST) + b2_ref[0]))
    h = jnp.dot(h, w3_ref[0], preferred_element_type=F32, precision=HIGHEST)
    dec = dec_ref[...]
    hf = h[:, :D_HY] * dec
    row = lax.broadcasted_iota(jnp.int32, (seq, 1), 0)
    hb = jnp.where(row > 0, h[:, D_HY:] * dec, 0.0)
    gp = (hf + hb).astype(BF16)
    gm = (hf - hb).astype(BF16)
    cm, sm = cm_ref[...], sm_ref[...]
    ch, sh = ch_ref[...], sh_ref[...]
    scale = 1.0 / seq
    o_ref[0, 0] = (_bdot(cm, gp) * ch + _bdot(sm, gp) * sh) * scale
    o_ref[0, 1] = (_bdot(cm, gm) * sh - _bdot(sm, gm) * ch) * scale


def _hyena_filters(seq, w1p, b1p, frp, w2p, b2p, w3, cm, sm):
    layers = w1p.shape[0]
    zp, decay = _filter_tables(seq)
    _, _, ch, sh = _dft_tables(seq)
    out = pl.pallas_call(
        functools.partial(_filter_kernel, seq=seq),
        out_shape=jax.ShapeDtypeStruct((layers * HY_ORDER, 2, seq, D_HY), F32),
        grid=(layers, HY_ORDER),
        in_specs=[_resident((seq, LANES)),
                  pl.BlockSpec((1, LANES, LANES), lambda l, o: (l, 0, 0)),
                  pl.BlockSpec((1, 1, LANES), lambda l, o: (l, 0, 0)),
                  pl.BlockSpec((1, 1, LANES), lambda l, o: (l, 0, 0)),
                  pl.BlockSpec((1, LANES, LANES), lambda l, o: (l, 0, 0)),
                  pl.BlockSpec((1, 1, LANES), lambda l, o: (l, 0, 0)),
                  pl.BlockSpec((1, LANES, 2 * D_HY), lambda l, o: (l, 0, o)),
                  _resident((seq, D_HY)), _resident((seq, seq)), _resident((seq, seq)),
                  _resident((seq, 1)), _resident((seq, 1))],
        out_specs=pl.BlockSpec((1, 2, seq, D_HY), lambda l, o: (l * HY_ORDER + o, 0, 0, 0)),
        compiler_params=_params(("parallel", "parallel")),
        name=f"hyena_filter_spectra_{seq}",
    )(jnp.asarray(zp), w1p, b1p, frp, w2p, b2p, w3, jnp.asarray(decay), cm, sm, jnp.asarray(ch), jnp.asarray(sh))
    return out.reshape(layers, 2 * HY_ORDER, seq, D_HY)


def _prenorm(x_ref, mod_ref, g_ref, shift_col, scale_col):
    d = D_MODEL
    shift = mod_ref[0, :, shift_col * d:(shift_col + 1) * d]
    scale = mod_ref[0, :, scale_col * d:(scale_col + 1) * d]
    return (_rmsnorm_rows(x_ref[...], g_ref[...]) * (1.0 + scale) + shift).astype(BF16)


def _qk_head_blocks(p, gain, rope):
    out = []
    for h in range(ATT_HEADS):
        xb = p[:, h * LANES:(h + 1) * LANES]
        xb = xb * lax.rsqrt(_group_sum(xb * xb) * (1.0 / QK_DIM) + EPS) * gain
        if rope is not None:
            cos, sin_lo, sin_hi = rope
            xb = (xb * cos + pltpu.roll(xb, LANES - ROPE_PAIRS, 1) * sin_lo
                  + pltpu.roll(xb, ROPE_PAIRS, 1) * sin_hi)
        out.append(xb)
    return jnp.concatenate(out, axis=-1)


def _in_kernel(*refs, tm, use_rope):
    if use_rope:
        (x_ref, mod_ref, g_ref, w_ref, cos_ref, slo_ref, shi_ref, qn_ref, kn_ref, lng_ref, lnb_ref,
         wcat_ref, bst_ref, pa_ref, osg_ref, q_ref, k_ref, v_ref) = refs
        rope = (cos_ref[...], slo_ref[...], shi_ref[...])
    else:
        (x_ref, mod_ref, g_ref, w_ref, qn_ref, kn_ref, lng_ref, lnb_ref,
         wcat_ref, bst_ref, pa_ref, osg_ref, q_ref, k_ref, v_ref) = refs
        rope = None
    hb = _prenorm(x_ref, mod_ref, g_ref, 0, 1)

    pa_ref[...] = _bdot(hb, w_ref[:, :IN_A])

    pb = jax.nn.gelu(_bdot(hb, w_ref[:, IN_A:IN_A + IN_B]))
    u = pb[:, :D_SG]
    vn = []
    for j in range(D_SG // LANES):
        xb = pb[:, D_SG + j * LANES:D_SG + (j + 1) * LANES]
        xc = xb - _group_sum(xb) * (1.0 / QK_DIM)
        vn.append(xc * lax.rsqrt(_group_sum(xc * xc) * (1.0 / QK_DIM) + EPS))
    vn = (jnp.concatenate(vn, axis=-1) * lng_ref[...] + lnb_ref[...]).astype(BF16)
    lane = lax.broadcasted_iota(jnp.int32, (1, D_SG), 1)
    grp = lane // (D_SG // SG_GROUPS)
    zero = jnp.zeros((SG_CHUNK, D_SG), BF16)
    for ci in range(tm // SG_CHUNK):
        rows = slice(ci * SG_CHUNK, (ci + 1) * SG_CHUNK)
        vc = vn[rows]
        stacked = jnp.concatenate([jnp.where(grp == g, vc, zero) for g in range(SG_GROUPS)], axis=0)
        s = _bdot(wcat_ref[...], stacked) + bst_ref[...]
        osg_ref[rows, :] = (u[rows] * s).astype(BF16)

    c0 = IN_A + IN_B
    q = _qk_head_blocks(_bdot(hb, w_ref[:, c0:c0 + D_QK]), qn_ref[...], rope)
    q_ref[...] = (q * (QK_DIM ** -0.5)).astype(BF16)
    k = _qk_head_blocks(_bdot(hb, w_ref[:, c0 + D_QK:c0 + 2 * D_QK]), kn_ref[...], rope)
    k_ref[...] = k.astype(BF16)
    v_ref[...] = _bdot(hb, w_ref[:, c0 + 2 * D_QK:]).astype(BF16)


def _in_kv_kernel(x_ref, mod_ref, g_ref, w_ref, kn_ref, k_ref, v_ref):
    hb = _prenorm(x_ref, mod_ref, g_ref, 0, 1)
    k = _qk_head_blocks(_bdot(hb, w_ref[:, :D_QK]), kn_ref[...], None)
    k_ref[...] = k.astype(BF16)
    v_ref[...] = _bdot(hb, w_ref[:, D_QK:]).astype(BF16)


def _mod_spec(mod_row):
    return pl.BlockSpec((1, 1, 6 * D_MODEL), lambda i: (mod_row(i), 0, 0))


def _in_proj(xs, modl, g1, w_in_b, qn, kn, lng, lnb, wcat, bstab, *, seq, tm, use_rope, mod_row):
    n, d = xs.shape
    row = lambda width: pl.BlockSpec((tm, width), lambda i: (i, 0))
    vec = lambda width: pl.BlockSpec((1, width), lambda i: (0, 0))
    in_specs = [row(d), _mod_spec(mod_row), vec(d), _resident((d, D_IN))]
    args = [xs, modl, g1, w_in_b]
    if use_rope:
        nblk = seq // tm
        in_specs += [pl.BlockSpec((tm, LANES), lambda i: (i % nblk, 0))] * 3
        args += [jnp.asarray(t) for t in _rope_tables(seq)]
    in_specs += [vec(LANES), vec(LANES), vec(D_SG), vec(D_SG), _resident((SG_CHUNK, SG_GROUPS * SG_CHUNK)),
                 _resident((SG_CHUNK, D_SG))]
    args += [qn, kn, lng, lnb, wcat, bstab]
    return pl.pallas_call(
        functools.partial(_in_kernel, tm=tm, use_rope=use_rope),
        out_shape=(jax.ShapeDtypeStruct((n, IN_A), F32), jax.ShapeDtypeStruct((n, D_SG), BF16),
                   jax.ShapeDtypeStruct((n, D_QK), BF16), jax.ShapeDtypeStruct((n, D_QK), BF16),
                   jax.ShapeDtypeStruct((n, D_ATT), BF16)),
        grid=(n // tm,),
        in_specs=in_specs,
        out_specs=(row(IN_A), row(D_SG), row(D_QK), row(D_QK), row(D_ATT)),
        compiler_params=_params(("parallel",)),
        name="in_proj_rope" if use_rope else "in_proj",
    )(*args)


def _in_proj_kv(xs, modl, g1, w_kv_b, kn, *, tm, mod_row):
    n, d = xs.shape
    row = lambda width: pl.BlockSpec((tm, width), lambda i: (i, 0))
    vec = lambda width: pl.BlockSpec((1, width), lambda i: (0, 0))
    return pl.pallas_call(
        _in_kv_kernel,
        out_shape=(jax.ShapeDtypeStruct((n, D_QK), BF16), jax.ShapeDtypeStruct((n, D_ATT), BF16)),
        grid=(n // tm,),
        in_specs=[row(d), _mod_spec(mod_row), vec(d), _resident((d, D_QK + D_ATT)), vec(LANES)],
        out_specs=(row(D_QK), row(D_ATT)),
        compiler_params=_params(("parallel",)),
        name="in_proj_kv",
    )(xs, modl, g1, w_kv_b, kn)


def _att_kernel(*refs, segments, lam_init):
    lam_ref, q_ref = refs[0], refs[1]
    kv_refs = refs[2:2 + 2 * segments]
    g_ref, o_ref = refs[2 + 2 * segments], refs[3 + 2 * segments]
    lv = lam_ref[...]
    lam = (jnp.exp(jnp.sum(lv[0:1] * lv[1:2], axis=-1, keepdims=True))
           - jnp.exp(jnp.sum(lv[2:3] * lv[3:4], axis=-1, keepdims=True)) + lam_init)
    lane = lax.broadcasted_iota(jnp.int32, (1, LANES), 1)
    nt = (((1,), (1,)), ((), ()))
    for h in range(ATT_HEADS):
        cols = slice(h * LANES, (h + 1) * LANES)
        qh = q_ref[0, :, cols]
        out = None
        for comp in range(2):
            keep = (lane < QK_DIM) if comp == 0 else (lane >= QK_DIM)
            qm = jnp.where(keep, qh, jnp.zeros_like(qh))
            s = [lax.dot_general(qm, kv_refs[2 * i][0, :, cols], nt, preferred_element_type=F32)
                 for i in range(segments)]
            m = jnp.max(s[0], axis=-1, keepdims=True)
            for si in s[1:]:
                m = jnp.maximum(m, jnp.max(si, axis=-1, keepdims=True))
            den, acc = None, None
            for i in range(segments):
                p = jnp.exp(s[i] - m)
                d_i = jnp.sum(p, axis=-1, keepdims=True)
                a_i = _bdot(p.astype(BF16), kv_refs[2 * i + 1][0, :, cols])
                den = d_i if den is None else den + d_i
                acc = a_i if acc is None else acc + a_i
            o_c = acc * (1.0 / den)
            out = o_c if comp == 0 else out - lam * o_c
        o_ref[0, :, cols] = (_rmsnorm_rows(out, g_ref[...]) * (1.0 - lam_init)).astype(BF16)


def _attention(lamv, q, kvs, subg, *, tq, lam_init):
    b, t, _ = q.shape
    in_specs = [pl.BlockSpec((4, LANES), lambda i, j: (0, 0)),
                pl.BlockSpec((1, tq, D_QK), lambda i, j: (i, j, 0))]
    args = [lamv, q]
    for k, v in kvs:
        tk = k.shape[1]
        in_specs += [pl.BlockSpec((1, tk, D_QK), lambda i, j: (i, 0, 0)),
                     pl.BlockSpec((1, tk, D_ATT), lambda i, j: (i, 0, 0))]
        args += [k, v]
    in_specs.append(pl.BlockSpec((1, V_DIM), lambda i, j: (0, 0)))
    args.append(subg)
    return pl.pallas_call(
        functools.partial(_att_kernel, segments=len(kvs), lam_init=lam_init),
        out_shape=jax.ShapeDtypeStruct((b, t, D_ATT), BF16),
        grid=(b, t // tq),
        in_specs=in_specs,
        out_specs=pl.BlockSpec((1, tq, D_ATT), lambda i, j: (i, j, 0)),
        compiler_params=_params(("parallel", "parallel")),
        name=f"diff_attention_{len(kvs)}seg",
    )(*args)


def _hyena_kernel(pa_ref, cw_ref, cb_ref, cm_ref, sm_ref, kt_ref, bias_ref, o_ref,
                  p_scr, z_scr, zb_scr, pb_scr, qb_scr, *, seq, rb):
    row = lax.broadcasted_iota(jnp.int32, (seq, 1), 0)
    for j in range(IN_A // D_HY):
        cols = slice(j * D_HY, (j + 1) * D_HY)
        pa = pa_ref[0, :, cols]
        prev = jnp.where(row > 0, pltpu.roll(pa, 1, 0), 0.0)
        nxt = jnp.where(row < seq - 1, pltpu.roll(pa, seq - 1, 0), 0.0)
        p_scr[:, cols] = (prev * cw_ref[0:1, cols] + pa * cw_ref[1:2, cols] + nxt * cw_ref[2:3, cols]
                          + cb_ref[:, cols])

    z_scr[...] = p_scr[:, :D_HY]
    for o in range(HY_ORDER):
        zb_scr[...] = z_scr[...].astype(BF16)
        gate_cols = slice((o + 1) * D_HY, (o + 2) * D_HY)

        def spectrum(i, carry):
            rows = pl.ds(pl.multiple_of(i * rb, rb), rb)
            a = _bdot(cm_ref[rows, :], zb_scr[...])
            b = _bdot(sm_ref[rows, :], zb_scr[...])
            kre, kim = kt_ref[2 * o, rows, :], kt_ref[2 * o + 1, rows, :]
            pb_scr[rows, :] = (a * kre + b * kim).astype(BF16)
            qb_scr[rows, :] = (b * kre - a * kim).astype(BF16)
            return carry

        lax.fori_loop(0, seq // rb, spectrum, 0)

        def synth(i, carry):
            rows = pl.ds(pl.multiple_of(i * rb, rb), rb)
            y = _bdot(cm_ref[rows, :], pb_scr[...]) + _bdot(sm_ref[rows, :], qb_scr[...])
            z_scr[rows, :] = p_scr[rows, gate_cols] * (y + bias_ref[o:o + 1, :] * z_scr[rows, :])
            return carry

        lax.fori_loop(0, seq // rb, synth, 0)
    o_ref[0] = z_scr[...].astype(BF16)


def _hyena(pa, cw, cb, cm, sm, ktab, bias, *, rb):
    b, seq, _ = pa.shape
    return pl.pallas_call(
        functools.partial(_hyena_kernel, seq=seq, rb=rb),
        out_shape=jax.ShapeDtypeStruct((b, seq, D_HY), BF16),
        grid=(b,),
        in_specs=[pl.BlockSpec((1, seq, IN_A), lambda i: (i, 0, 0), pipeline_mode=pl.Buffered(1)),
                  _resident((3, IN_A)), _resident((1, IN_A)), _resident((seq, seq)), _resident((seq, seq)),
                  _resident((2 * HY_ORDER, seq, D_HY)), _resident((HY_ORDER, D_HY))],
        out_specs=pl.BlockSpec((1, seq, D_HY), lambda i: (i, 0, 0)),
        scratch_shapes=[pltpu.VMEM((seq, IN_A), F32), pltpu.VMEM((seq, D_HY), F32),
                        pltpu.VMEM((seq, D_HY), BF16), pltpu.VMEM((seq, D_HY), BF16),
                        pltpu.VMEM((seq, D_HY), BF16)],
        compiler_params=_params(("parallel",)),
        name=f"hyena_mixer_{seq}",
    )(pa, cw, cb, cm, sm, ktab, bias)


def _out_kernel(ohy_ref, osg_ref, oat_ref, x_ref, mod_ref, g2_ref, w_ref, nx_ref, h2_ref):
    d = D_MODEL
    mix = (_bdot(ohy_ref[...], w_ref[:D_HY]) + _bdot(osg_ref[...], w_ref[D_HY:D_HY + D_SG])
           + _bdot(oat_ref[...], w_ref[D_HY + D_SG:]))
    nx = x_ref[...] + mod_ref[0, :, 2 * d:3 * d] * mix
    nx_ref[...] = nx
    h2 = _rmsnorm_rows(nx, g2_ref[...]) * (1.0 + mod_ref[0, :, 4 * d:5 * d]) + mod_ref[0, :, 3 * d:4 * d]
    h2_ref[...] = h2.astype(BF16)


def _out_proj(ohy, osg, oat, xs, modl, g2, w_out_b, *, tm, mod_row):
    n, d = xs.shape
    row = lambda width: pl.BlockSpec((tm, width), lambda i: (i, 0))
    return pl.pallas_call(
        _out_kernel,
        out_shape=(jax.ShapeDtypeStruct((n, d), F32), jax.ShapeDtypeStruct((n, d), BF16)),
        grid=(n // tm,),
        in_specs=[row(D_HY), row(D_SG), row(D_ATT), row(d), _mod_spec(mod_row),
                  pl.BlockSpec((1, d), lambda i: (0, 0)), _resident((D_MIX, d))],
        out_specs=(row(d), row(d)),
        compiler_params=_params(("parallel",)),
        name="out_proj",
    )(ohy, osg, oat, xs, modl, g2, w_out_b)


def _ffn_kernel(h_ref, hp_ref, hn_ref, nx_ref, mod_ref, wu_ref, cw_ref, cb_ref, wd_ref, o_ref, *, tm, seq):
    d = D_MODEL
    nchunk = D_FF // FF_CHUNK
    rows_ext = tm + 2 * HALO
    hcat = jnp.concatenate([hp_ref[...], h_ref[...], hn_ref[...]], axis=0)
    pos = (pl.program_id(0) * tm + lax.broadcasted_iota(jnp.int32, (tm, 1), 0)) % seq
    has_prev = pos > 0
    has_next = pos < seq - 1

    def conv(u, j):
        w = cw_ref[j]
        prev = jnp.where(has_prev, pltpu.roll(u, 1, 0)[HALO:HALO + tm], 0.0)
        nxt = jnp.where(has_next, pltpu.roll(u, rows_ext - 1, 0)[HALO:HALO + tm], 0.0)
        return prev * w[0:1] + u[HALO:HALO + tm] * w[1:2] + nxt * w[2:3] + cb_ref[j]

    o_ref[...] = jnp.zeros_like(o_ref)

    def chunk(j, carry):
        gate = conv(_bdot(hcat, wu_ref[j]), j)
        val = conv(_bdot(hcat, wu_ref[nchunk + j]), nchunk + j)
        act = (gate * jax.nn.sigmoid(gate) * val).astype(BF16)
        o_ref[...] += _bdot(act, wd_ref[j])
        return carry

    lax.fori_loop(0, nchunk, chunk, 0)
    o_ref[...] = nx_ref[...] + mod_ref[0, :, 5 * d:6 * d] * o_ref[...]


def _ffn(h2, nx, modl, wu, cw, cb, wd, *, tm, seq, mod_row):
    n, d = nx.shape
    nh = n // HALO
    per = tm // HALO
    row = lambda width: pl.BlockSpec((tm, width), lambda i: (i, 0))
    return pl.pallas_call(
        functools.partial(_ffn_kernel, tm=tm, seq=seq),
        out_shape=jax.ShapeDtypeStruct((n, d), F32),
        grid=(n // tm,),
        in_specs=[row(d),
                  pl.BlockSpec((HALO, d), lambda i: (jnp.maximum(i * per - 1, 0), 0)),
                  pl.BlockSpec((HALO, d), lambda i: (jnp.minimum((i + 1) * per, nh - 1), 0)),
                  row(d), _mod_spec(mod_row),
                  _resident(wu.shape), _resident(cw.shape), _resident(cb.shape), _resident(wd.shape)],
        out_specs=row(d),
        compiler_params=_params(("parallel",)),
        name="conv_ffn",
    )(h2, h2, h2, nx, modl, wu, cw, cb, wd)


def _pad_to(a, shape):
    return jnp.pad(a, [(0, t - s) for s, t in zip(a.shape, shape)])


def kernel(x, c, ctx, c_ctx, w_mod, b_mod, norm1_g, w_in, hy_conv_w, hy_conv_b, hy_w1, hy_b1, hy_w2, hy_b2,
           hy_w3, hy_freq, hy_bias, sg_ln_g, sg_ln_b, sg_w, sg_b, q_norm_g, k_norm_g, lam_q1, lam_k1, lam_q2,
           lam_k2, subln_g, w_out, norm2_g, ffn_w_up, ffn_conv_w, ffn_conv_b, ffn_w_down):
    bsz, seq, d = x.shape
    lc = ctx.shape[1]
    depth = w_mod.shape[0]
    assert d == D_MODEL and bsz + 1 <= MOD_ROWS and (bsz * lc) % 512 == 0 and seq % 512 == 0

    cc = _pad_to(jnp.concatenate([c, c_ctx[None, :]], axis=0), (MOD_ROWS, d))
    mod = _modulation(cc, w_mod, b_mod)
    ctx_row = bsz

    w1p = _pad_to(hy_w1, (depth, LANES, LANES))
    b1p = _pad_to(hy_b1, (depth, LANES)).reshape(depth, 1, LANES)
    frp = _pad_to(hy_freq, (depth, LANES)).reshape(depth, 1, LANES)
    w2p = _pad_to(hy_w2, (depth, LANES, LANES))
    b2p = _pad_to(hy_b2, (depth, LANES)).reshape(depth, 1, LANES)
    w3p = _pad_to(hy_w3, (depth, LANES, HY_ORDER * 2 * D_HY))
    dft = {}
    for s in (seq, lc):
        cmat, smat, _, _ = _dft_tables(s)
        dft[s] = (jnp.asarray(cmat).astype(BF16), jnp.asarray(smat).astype(BF16))
    ktab = _hyena_filters(seq, w1p, b1p, frp, w2p, b2p, w3p, *dft[seq])
    nctx = depth - 1
    ktab_c = _hyena_filters(lc, w1p[:nctx], b1p[:nctx], frp[:nctx], w2p[:nctx], b2p[:nctx], w3p[:nctx], *dft[lc])

    tm = 512
    nchunk = D_FF // FF_CHUNK
    xs = x.reshape(bsz * seq, d)
    cs = ctx.reshape(bsz * lc, d)
    lat_row = lambda i: (i * tm) // seq
    ctx_mod = lambda i: ctx_row

    for l in range(depth):
        last = l == depth - 1
        lam_init = 0.8 - 0.6 * math.exp(-0.3 * l)
        modl = mod[l].reshape(MOD_ROWS, 1, 6 * d)
        g1 = norm1_g[l].reshape(1, d)
        g2 = norm2_g[l].reshape(1, d)
        w_in_b = w_in[l].astype(BF16)
        w_out_b = w_out[l].astype(BF16)
        wu = ffn_w_up[l].astype(BF16).reshape(d, 2 * nchunk, FF_CHUNK).transpose(1, 0, 2)
        wd = ffn_w_down[l].astype(BF16).reshape(nchunk, FF_CHUNK, d)
        fcw = ffn_conv_w[l].reshape(3, 2 * nchunk, FF_CHUNK).transpose(1, 0, 2)
        fcb = ffn_conv_b[l].reshape(2 * nchunk, 1, FF_CHUNK)
        qn = jnp.tile(q_norm_g[l], LANES // QK_DIM).reshape(1, LANES)
        kn = jnp.tile(k_norm_g[l], LANES // QK_DIM).reshape(1, LANES)
        lng = sg_ln_g[l].reshape(1, D_SG)
        lnb = sg_ln_b[l].reshape(1, D_SG)
        wcat = sg_w[l].transpose(1, 0, 2).reshape(SG_CHUNK, SG_GROUPS * SG_CHUNK).astype(BF16)
        bstab = jnp.repeat(sg_b[l].T, D_SG // SG_GROUPS, axis=1)
        lamv = _pad_to(jnp.stack([lam_q1[l], lam_k1[l], lam_q2[l], lam_k2[l]]), (4, LANES))
        subg = subln_g[l].reshape(1, V_DIM)
        hcw = hy_conv_w[l]
        hcb = hy_conv_b[l].reshape(1, IN_A)
        sgu = (qn, kn, lng, lnb, wcat, bstab)

        pa, osg, q, k, v = _in_proj(xs, modl, g1, w_in_b, *sgu, seq=seq, tm=tm, use_rope=True, mod_row=lat_row)
        if last:
            kc, vc = _in_proj_kv(cs, modl, g1, w_in_b[:, IN_A + IN_B + D_QK:], kn, tm=tm, mod_row=ctx_mod)
        else:
            pa_c, osg_c, qc, kc, vc = _in_proj(cs, modl, g1, w_in_b, *sgu, seq=lc, tm=tm, use_rope=False,
                                               mod_row=ctx_mod)
        b3 = lambda a, t: a.reshape(bsz, t, a.shape[-1])
        kc3, vc3 = b3(kc, lc), b3(vc, lc)
        oat = _attention(lamv, b3(q, seq), [(b3(k, seq), b3(v, seq)), (kc3, vc3)], subg, tq=256, lam_init=lam_init)
        ohy = _hyena(b3(pa, seq), hcw, hcb, *dft[seq], ktab[l], hy_bias[l], rb=512)
        nx, h2 = _out_proj(ohy.reshape(-1, D_HY), osg, oat.reshape(-1, D_ATT), xs, modl, g2, w_out_b,
                           tm=tm, mod_row=lat_row)
        xs_next = _ffn(h2, nx, modl, wu, fcw, fcb, wd, tm=tm, seq=seq, mod_row=lat_row)

        if not last:
            oat_c = _attention(lamv, b3(qc, lc), [(kc3, vc3)], subg, tq=lc, lam_init=lam_init)
            ohy_c = _hyena(b3(pa_c, lc), hcw, hcb, *dft[lc], ktab_c[l], hy_bias[l], rb=lc)
            nx_c, h2_c = _out_proj(ohy_c.reshape(-1, D_HY), osg_c, oat_c.reshape(-1, D_ATT), cs, modl, g2,
                                   w_out_b, tm=tm, mod_row=ctx_mod)
            cs = _ffn(h2_c, nx_c, modl, wu, fcw, fcb, wd, tm=tm, seq=lc, mod_row=ctx_mod)
        xs = xs_next
    return xs.reshape(bsz, seq, d)
```

```python
import functools
import math

import jax
import jax.numpy as jnp
import numpy as np
from jax import lax
from jax.experimental import pallas as pl
from jax.experimental.pallas import tpu as pltpu

F32 = jnp.float32
BF16 = jnp.bfloat16
HIGHEST = lax.Precision.HIGHEST

D_MODEL = 1024
DEPTH = 2
GRID_W = 64
EPS = 1e-6
D_HY = 256
HY_ORDER = 2
HY_BANDS = 16
HY_EMB = 1 + 2 * HY_BANDS
HY_FFN = 64
HY_DECAY_TARGET = 1e-2
HY_FAST_DECAY = 0.3
HY_SLOW_DECAY = 1.5
SG_GROUPS = 4
D_SG = 256
SG_CHUNK = 128
ATT_HEADS = 4
QK_DIM = 64
V_DIM = 128
D_ATT = ATT_HEADS * V_DIM
ROPE_PAIRS = QK_DIM // 4
ROPE_BASE = 10000.0
D_MIX = D_HY + D_SG + D_ATT
IN_A = 3 * D_HY
IN_B = 2 * D_SG
D_QK = 2 * ATT_HEADS * QK_DIM
IN_C = 2 * D_QK + D_ATT
D_IN = IN_A + IN_B + IN_C
D_FF = 2816
D_VAUG = 2 * D_ATT
LOG2E = 1.4426950408889634

LANES = 128
BF16_SUBLANES = 16
VMEM_LIMIT_BYTES = 56 * 1024 * 1024

MOD_ROWS = 16
FF_CHUNK = 256
ATT_KEY_PIECES = 2
HALO = BF16_SUBLANES


def _params(sem, vmem=VMEM_LIMIT_BYTES):
    return pltpu.CompilerParams(dimension_semantics=sem, vmem_limit_bytes=vmem)


def _resident(shape):
    nd = len(shape)
    return pl.BlockSpec(shape, lambda *_: (0,) * nd, pipeline_mode=pl.Buffered(1))


@functools.lru_cache(maxsize=None)
def _dft_tables(seq):
    n2 = 2 * seq
    f = np.arange(seq, dtype=np.int64)
    m = ((2 * f[:, None] + 1) * (2 * f[None, :] + 1)) % (4 * n2)
    ang = (2.0 * np.pi / (4 * n2)) * m.astype(np.float64)
    half = np.pi * (2 * f + 1) / (2.0 * n2)
    return (np.cos(ang).astype(np.float32), np.sin(ang).astype(np.float32),
            np.cos(half).astype(np.float32)[:, None], np.sin(half).astype(np.float32)[:, None])


@functools.lru_cache(maxsize=None)
def _filter_tables(seq):
    t = np.linspace(0.0, 1.0, seq, dtype=np.float32)[:, None]
    t_r = np.arange(seq, dtype=np.float32)[:, None]
    bands = np.linspace(1e-4, HY_BANDS - 1, HY_BANDS, dtype=np.float32)[None, :]
    w = (2.0 * math.pi * t_r / seq).astype(np.float32)
    z = np.concatenate([t, np.cos(bands * w), -np.sin(bands * w)], axis=-1).astype(np.float32)
    zp = np.zeros((seq, LANES), np.float32)
    zp[:, :HY_EMB] = z
    min_decay = math.log(HY_DECAY_TARGET) / HY_SLOW_DECAY
    max_decay = math.log(HY_DECAY_TARGET) / HY_FAST_DECAY
    deltas = np.abs(np.linspace(min_decay, max_decay, D_HY, dtype=np.float32))
    decay = np.exp(-t * deltas[None, :]).astype(np.float32)
    return zp, decay


@functools.lru_cache(maxsize=None)
def _rope_tables(seq):
    pos = np.arange(seq)
    row = (pos // GRID_W).astype(np.float32)
    col = (pos % GRID_W).astype(np.float32)
    inv = (ROPE_BASE ** (-np.arange(ROPE_PAIRS, dtype=np.float32) / ROPE_PAIRS)).astype(np.float32)
    lane = np.arange(LANES)
    axis = (lane % QK_DIM) // (2 * ROPE_PAIRS)
    half = (lane % (2 * ROPE_PAIRS)) // ROPE_PAIRS
    pair = lane % ROPE_PAIRS
    p = np.where(axis[None, :] == 0, row[:, None], col[:, None]).astype(np.float32)
    ang = (p * inv[pair][None, :]).astype(np.float32)
    cos, sin = np.cos(ang).astype(np.float32), np.sin(ang).astype(np.float32)
    sin_lo = np.where(half[None, :] == 0, -sin, 0.0).astype(np.float32)
    sin_hi = np.where(half[None, :] == 1, sin, 0.0).astype(np.float32)
    return cos, sin_lo, sin_hi


def _group_sum(x):
    lane = lax.broadcasted_iota(jnp.int32, (1, LANES), 1)
    lo = lane < QK_DIM
    s_lo = jnp.sum(jnp.where(lo, x, 0.0), axis=-1, keepdims=True)
    s_hi = jnp.sum(jnp.where(lo, 0.0, x), axis=-1, keepdims=True)
    return jnp.where(lo, s_lo, s_hi)


def _rmsnorm_rows(x, g):
    return x * lax.rsqrt(jnp.mean(x * x, axis=-1, keepdims=True) + EPS) * g


def _bdot(a, b):
    return jnp.dot(a, b, preferred_element_type=F32)


def _v_aug(v):
    ones = jnp.ones((v.shape[0], V_DIM), BF16)
    vb = v.astype(BF16)
    parts = []
    for h in range(ATT_HEADS):
        parts += [vb[:, h * V_DIM:(h + 1) * V_DIM], ones]
    return jnp.concatenate(parts, axis=-1)


def _mod_kernel(cc_ref, w_ref, b_ref, o_ref):
    cc = cc_ref[...]
    a = cc * jax.nn.sigmoid(cc)
    o_ref[0] = jnp.dot(a, w_ref[0], preferred_element_type=F32, precision=HIGHEST) + b_ref[0]


def _modulation(cc, w_mod, b_mod):
    depth, d, n = w_mod.shape
    tn = 768
    return pl.pallas_call(
        _mod_kernel,
        out_shape=jax.ShapeDtypeStruct((depth, MOD_ROWS, n), F32),
        grid=(depth, n // tn),
        in_specs=[pl.BlockSpec((MOD_ROWS, d), lambda l, j: (0, 0)),
                  pl.BlockSpec((1, d, tn), lambda l, j: (l, 0, j)),
                  pl.BlockSpec((1, 1, tn), lambda l, j: (l, 0, j))],
        out_specs=pl.BlockSpec((1, MOD_ROWS, tn), lambda l, j: (l, 0, j)),
        compiler_params=_params(("parallel", "parallel")),
        name="adaln_modulation",
    )(cc, w_mod, b_mod.reshape(depth, 1, n))


def _filter_kernel(z_ref, w1_ref, b1_ref, fr_ref, w2_ref, b2_ref, w3_ref, dec_ref, cm_ref, sm_ref,
                   ch_ref, sh_ref, o_ref, *, seq):
    fr = fr_ref[0]
    h = jnp.sin(fr * (jnp.dot(z_ref[...], w1_ref[0], preferred_element_type=F32, precision=HIGHEST) + b1_ref[0]))
    h = jnp.sin(fr * (jnp.dot(h, w2_ref[0], preferred_element_type=F32, precision=HIGHEST) + b2_ref[0]))
    h = jnp.dot(h, w3_ref[0], preferred_element_type=F32, precision=HIGHEST)
    dec = dec_ref[...]
    hf = h[:, :D_HY] * dec
    row = lax.broadcasted_iota(jnp.int32, (seq, 1), 0)
    hb = jnp.where(row > 0, h[:, D_HY:] * dec, 0.0)
    gp = (hf + hb).astype(BF16)
    gm = (hf - hb).astype(BF16)
    cm, sm = cm_ref[...], sm_ref[...]
    ch, sh = ch_ref[...], sh_ref[...]
    scale = 1.0 / seq
    o_ref[0, 0] = (_bdot(cm, gp) * ch + _bdot(sm, gp) * sh) * scale
    o_ref[0, 1] = (_bdot(cm, gm) * sh - _bdot(sm, gm) * ch) * scale


def _hyena_filters(seq, w1p, b1p, frp, w2p, b2p, w3, cm, sm):
    layers = w1p.shape[0]
    zp, decay = _filter_tables(seq)
    _, _, ch, sh = _dft_tables(seq)
    out = pl.pallas_call(
        functools.partial(_filter_kernel, seq=seq),
        out_shape=jax.ShapeDtypeStruct((layers * HY_ORDER, 2, seq, D_HY), F32),
        grid=(layers, HY_ORDER),
        in_specs=[_resident((seq, LANES)),
                  pl.BlockSpec((1, LANES, LANES), lambda l, o: (l, 0, 0)),
                  pl.BlockSpec((1, 1, LANES), lambda l, o: (l, 0, 0)),
                  pl.BlockSpec((1, 1, LANES), lambda l, o: (l, 0, 0)),
                  pl.BlockSpec((1, LANES, LANES), lambda l, o: (l, 0, 0)),
                  pl.BlockSpec((1, 1, LANES), lambda l, o: (l, 0, 0)),
                  pl.BlockSpec((1, LANES, 2 * D_HY), lambda l, o: (l, 0, o)),
                  _resident((seq, D_HY)), _resident((seq, seq)), _resident((seq, seq)),
                  _resident((seq, 1)), _resident((seq, 1))],
        out_specs=pl.BlockSpec((1, 2, seq, D_HY), lambda l, o: (l * HY_ORDER + o, 0, 0, 0)),
        compiler_params=_params(("parallel", "parallel")),
        name=f"hyena_filter_spectra_{seq}",
    )(jnp.asarray(zp), w1p, b1p, frp, w2p, b2p, w3, jnp.asarray(decay), cm, sm, jnp.asarray(ch), jnp.asarray(sh))
    return out.reshape(layers, 2 * HY_ORDER, seq, D_HY)


def _prenorm(x_ref, mod_ref, g_ref, shift_col, scale_col):
    d = D_MODEL
    shift = mod_ref[0, :, shift_col * d:(shift_col + 1) * d]
    scale = mod_ref[0, :, scale_col * d:(scale_col + 1) * d]
    return (_rmsnorm_rows(x_ref[...], g_ref[...]) * (1.0 + scale) + shift).astype(BF16)


def _qk_head_blocks(p, gain, rope):
    out = []
    for h in range(ATT_HEADS):
        xb = p[:, h * LANES:(h + 1) * LANES]
        xb = xb * lax.rsqrt(_group_sum(xb * xb) * (1.0 / QK_DIM) + EPS) * gain
        if rope is not None:
            cos, sin_lo, sin_hi = rope
            xb = (xb * cos + pltpu.roll(xb, LANES - ROPE_PAIRS, 1) * sin_lo
                  + pltpu.roll(xb, ROPE_PAIRS, 1) * sin_hi)
        out.append(xb)
    return jnp.concatenate(out, axis=-1)


def _in_kernel(*refs, tm, use_rope):
    if use_rope:
        (x_ref, mod_ref, g_ref, w_ref, cos_ref, slo_ref, shi_ref, qn_ref, kn_ref, lng_ref, lnb_ref,
         wcat_ref, bst_ref, pa_ref, osg_ref, q_ref, k_ref, v_ref) = refs
        rope = (cos_ref[...], slo_ref[...], shi_ref[...])
    else:
        (x_ref, mod_ref, g_ref, w_ref, qn_ref, kn_ref, lng_ref, lnb_ref,
         wcat_ref, bst_ref, pa_ref, osg_ref, q_ref, k_ref, v_ref) = refs
        rope = None
    hb = _prenorm(x_ref, mod_ref, g_ref, 0, 1)

    pa_ref[...] = _bdot(hb, w_ref[:, :IN_A])

    pb = jax.nn.gelu(_bdot(hb, w_ref[:, IN_A:IN_A + IN_B]))
    u = pb[:, :D_SG]
    vn = []
    for j in range(D_SG // LANES):
        xb = pb[:, D_SG + j * LANES:D_SG + (j + 1) * LANES]
        xc = xb - _group_sum(xb) * (1.0 / QK_DIM)
        vn.append(xc * lax.rsqrt(_group_sum(xc * xc) * (1.0 / QK_DIM) + EPS))
    vn = (jnp.concatenate(vn, axis=-1) * lng_ref[...] + lnb_ref[...]).astype(BF16)
    lane = lax.broadcasted_iota(jnp.int32, (1, D_SG), 1)
    grp = lane // (D_SG // SG_GROUPS)
    zero = jnp.zeros((SG_CHUNK, D_SG), BF16)
    for ci in range(tm // SG_CHUNK):
        rows = slice(ci * SG_CHUNK, (ci + 1) * SG_CHUNK)
        vc = vn[rows]
        stacked = jnp.concatenate([jnp.where(grp == g, vc, zero) for g in range(SG_GROUPS)], axis=0)
        s = _bdot(wcat_ref[...], stacked) + bst_ref[...]
        osg_ref[rows, :] = (u[rows] * s).astype(BF16)

    c0 = IN_A + IN_B
    q = _qk_head_blocks(_bdot(hb, w_ref[:, c0:c0 + D_QK]), qn_ref[...], rope)
    q_ref[...] = (q * (QK_DIM ** -0.5 * LOG2E)).astype(BF16)
    k = _qk_head_blocks(_bdot(hb, w_ref[:, c0 + D_QK:c0 + 2 * D_QK]), kn_ref[...], rope)
    k_ref[...] = k.astype(BF16)
    v_ref[...] = _v_aug(_bdot(hb, w_ref[:, c0 + 2 * D_QK:]))


def _in_kv_kernel(x_ref, mod_ref, g_ref, w_ref, kn_ref, k_ref, v_ref):
    hb = _prenorm(x_ref, mod_ref, g_ref, 0, 1)
    k = _qk_head_blocks(_bdot(hb, w_ref[:, :D_QK]), kn_ref[...], None)
    k_ref[...] = k.astype(BF16)
    v_ref[...] = _v_aug(_bdot(hb, w_ref[:, D_QK:]))


def _mod_spec(mod_row):
    return pl.BlockSpec((1, 1, 6 * D_MODEL), lambda i: (mod_row(i), 0, 0))


def _in_proj(xs, modl, g1, w_in_b, qn, kn, lng, lnb, wcat, bstab, *, seq, tm, use_rope, mod_row):
    n, d = xs.shape
    row = lambda width: pl.BlockSpec((tm, width), lambda i: (i, 0))
    vec = lambda width: pl.BlockSpec((1, width), lambda i: (0, 0))
    in_specs = [row(d), _mod_spec(mod_row), vec(d), _resident((d, D_IN))]
    args = [xs, modl, g1, w_in_b]
    if use_rope:
        nblk = seq // tm
        in_specs += [pl.BlockSpec((tm, LANES), lambda i: (i % nblk, 0))] * 3
        args += [jnp.asarray(t) for t in _rope_tables(seq)]
    in_specs += [vec(LANES), vec(LANES), vec(D_SG), vec(D_SG), _resident((SG_CHUNK, SG_GROUPS * SG_CHUNK)),
                 _resident((SG_CHUNK, D_SG))]
    args += [qn, kn, lng, lnb, wcat, bstab]
    return pl.pallas_call(
        functools.partial(_in_kernel, tm=tm, use_rope=use_rope),
        out_shape=(jax.ShapeDtypeStruct((n, IN_A), F32), jax.ShapeDtypeStruct((n, D_SG), BF16),
                   jax.ShapeDtypeStruct((n, D_QK), BF16), jax.ShapeDtypeStruct((n, D_QK), BF16),
                   jax.ShapeDtypeStruct((n, D_VAUG), BF16)),
        grid=(n // tm,),
        in_specs=in_specs,
        out_specs=(row(IN_A), row(D_SG), row(D_QK), row(D_QK), row(D_VAUG)),
        compiler_params=_params(("parallel",)),
        name="in_proj_rope" if use_rope else "in_proj",
    )(*args)


def _in_proj_kv(xs, modl, g1, w_kv_b, kn, *, tm, mod_row):
    n, d = xs.shape
    row = lambda width: pl.BlockSpec((tm, width), lambda i: (i, 0))
    vec = lambda width: pl.BlockSpec((1, width), lambda i: (0, 0))
    return pl.pallas_call(
        _in_kv_kernel,
        out_shape=(jax.ShapeDtypeStruct((n, D_QK), BF16), jax.ShapeDtypeStruct((n, D_VAUG), BF16)),
        grid=(n // tm,),
        in_specs=[row(d), _mod_spec(mod_row), vec(d), _resident((d, D_QK + D_ATT)), vec(LANES)],
        out_specs=(row(D_QK), row(D_VAUG)),
        compiler_params=_params(("parallel",)),
        name="in_proj_kv",
    )(xs, modl, g1, w_kv_b, kn)


def _att_kernel(*refs, segments, lam_init, tq):
    lam_ref, q_ref = refs[0], refs[1]
    kv_refs = refs[2:2 + 2 * segments]
    g_ref, o_ref = refs[2 + 2 * segments], refs[3 + 2 * segments]
    lv = lam_ref[...]
    lam = (jnp.exp(jnp.sum(lv[0:1] * lv[1:2], axis=-1, keepdims=True))
           - jnp.exp(jnp.sum(lv[2:3] * lv[3:4], axis=-1, keepdims=True)) + lam_init)
    lo = lax.broadcasted_iota(jnp.int32, (1, LANES), 1) < QK_DIM
    nt = (((1,), (1,)), ((), ()))

    def head_scores(h):
        cols = slice(h * LANES, (h + 1) * LANES)
        qh = q_ref[0, :, cols]
        zero = jnp.zeros_like(qh)
        return [[lax.dot_general(qm, kv_refs[2 * i][0, :, cols], nt, preferred_element_type=F32)
                 for i in range(segments)]
                for qm in (jnp.where(lo, qh, zero), jnp.where(lo, zero, qh))]

    nxt = head_scores(0)
    for h in range(ATT_HEADS):
        cols = slice(h * LANES, (h + 1) * LANES)
        scores, nxt = nxt, (head_scores(h + 1) if h + 1 < ATT_HEADS else None)
        o = []
        for s in scores:
            m = jnp.max(s[0], axis=-1, keepdims=True)
            for si in s[1:]:
                m = jnp.maximum(m, jnp.max(si, axis=-1, keepdims=True))
            acc = None
            for i in range(segments):
                p = jnp.exp2((s[i] - m).astype(BF16))
                a_i = _bdot(p, kv_refs[2 * i + 1][0, :, 2 * h * V_DIM:2 * (h + 1) * V_DIM])
                acc = a_i if acc is None else acc + a_i
            o.append(acc[:, :V_DIM] / acc[:, V_DIM:])
        out = o[0] - lam * o[1]
        o_ref[0, :, cols] = (_rmsnorm_rows(out, g_ref[...]) * (1.0 - lam_init)).astype(BF16)


def _attention(lamv, q, kvs, subg, *, tq, lam_init):
    b, t, _ = q.shape
    in_specs = [pl.BlockSpec((4, LANES), lambda i, j: (0, 0)),
                pl.BlockSpec((1, tq, D_QK), lambda i, j: (i, j, 0))]
    args = [lamv, q]
    for k, v, pieces in kvs:
        tk = k.shape[1] // pieces
        for piece in range(pieces):
            in_specs += [pl.BlockSpec((1, tk, D_QK), lambda i, j, piece=piece: (i, piece, 0)),
                         pl.BlockSpec((1, tk, D_VAUG), lambda i, j, piece=piece: (i, piece, 0))]
            args += [k, v]
    in_specs.append(pl.BlockSpec((1, V_DIM), lambda i, j: (0, 0)))
    args.append(subg)
    nseg = (len(in_specs) - 3) // 2
    return pl.pallas_call(
        functools.partial(_att_kernel, segments=nseg, lam_init=lam_init, tq=tq),
        out_shape=jax.ShapeDtypeStruct((b, t, D_ATT), BF16),
        grid=(b, t // tq),
        in_specs=in_specs,
        out_specs=pl.BlockSpec((1, tq, D_ATT), lambda i, j: (i, j, 0)),
        compiler_params=_params(("parallel", "parallel")),
        name=f"diff_attention_{nseg}seg",
    )(*args)


def _hyena_kernel(pa_ref, cw_ref, cb_ref, cm_ref, sm_ref, kt_ref, bias_ref, o_ref,
                  p_scr, z_scr, zb_scr, pb_scr, qb_scr, *, seq, rb):
    row = lax.broadcasted_iota(jnp.int32, (seq, 1), 0)
    for j in range(IN_A // D_HY):
        cols = slice(j * D_HY, (j + 1) * D_HY)
        pa = pa_ref[0, :, cols]
        prev = jnp.where(row > 0, pltpu.roll(pa, 1, 0), 0.0)
        nxt = jnp.where(row < seq - 1, pltpu.roll(pa, seq - 1, 0), 0.0)
        p_scr[:, cols] = (prev * cw_ref[0:1, cols] + pa * cw_ref[1:2, cols] + nxt * cw_ref[2:3, cols]
                          + cb_ref[:, cols])

    z_scr[...] = p_scr[:, :D_HY]
    for o in range(HY_ORDER):
        zb_scr[...] = z_scr[...].astype(BF16)
        gate_cols = slice((o + 1) * D_HY, (o + 2) * D_HY)

        def spectrum(i, carry):
            rows = pl.ds(pl.multiple_of(i * rb, rb), rb)
            a = _bdot(cm_ref[rows, :], zb_scr[...])
            b = _bdot(sm_ref[rows, :], zb_scr[...])
            kre, kim = kt_ref[2 * o, rows, :], kt_ref[2 * o + 1, rows, :]
            pb_scr[rows, :] = (a * kre + b * kim).astype(BF16)
            qb_scr[rows, :] = (b * kre - a * kim).astype(BF16)
            return carry

        lax.fori_loop(0, seq // rb, spectrum, 0)

        def synth(i, carry):
            rows = pl.ds(pl.multiple_of(i * rb, rb), rb)
            y = _bdot(cm_ref[rows, :], pb_scr[...]) + _bdot(sm_ref[rows, :], qb_scr[...])
            z_scr[rows, :] = p_scr[rows, gate_cols] * (y + bias_ref[o:o + 1, :] * z_scr[rows, :])
            return carry

        lax.fori_loop(0, seq // rb, synth, 0)
    o_ref[0] = z_scr[...].astype(BF16)


def _hyena(pa, cw, cb, cm, sm, ktab, bias, *, rb):
    b, seq, _ = pa.shape
    return pl.pallas_call(
        functools.partial(_hyena_kernel, seq=seq, rb=rb),
        out_shape=jax.ShapeDtypeStruct((b, seq, D_HY), BF16),
        grid=(b,),
        in_specs=[pl.BlockSpec((1, seq, IN_A), lambda i: (i, 0, 0), pipeline_mode=pl.Buffered(1)),
                  _resident((3, IN_A)), _resident((1, IN_A)), _resident((seq, seq)), _resident((seq, seq)),
                  _resident((2 * HY_ORDER, seq, D_HY)), _resident((HY_ORDER, D_HY))],
        out_specs=pl.BlockSpec((1, seq, D_HY), lambda i: (i, 0, 0)),
        scratch_shapes=[pltpu.VMEM((seq, IN_A), F32), pltpu.VMEM((seq, D_HY), F32),
                        pltpu.VMEM((seq, D_HY), BF16), pltpu.VMEM((seq, D_HY), BF16),
                        pltpu.VMEM((seq, D_HY), BF16)],
        compiler_params=_params(("parallel",)),
        name=f"hyena_mixer_{seq}",
    )(pa, cw, cb, cm, sm, ktab, bias)


def _out_kernel(ohy_ref, osg_ref, oat_ref, x_ref, mod_ref, g2_ref, w_ref, nx_ref, h2_ref):
    d = D_MODEL
    mix = (_bdot(ohy_ref[...], w_ref[:D_HY]) + _bdot(osg_ref[...], w_ref[D_HY:D_HY + D_SG])
           + _bdot(oat_ref[...], w_ref[D_HY + D_SG:]))
    nx = x_ref[...] + mod_ref[0, :, 2 * d:3 * d] * mix
    nx_ref[...] = nx
    h2 = _rmsnorm_rows(nx, g2_ref[...]) * (1.0 + mod_ref[0, :, 4 * d:5 * d]) + mod_ref[0, :, 3 * d:4 * d]
    h2_ref[...] = h2.astype(BF16)


def _out_proj(ohy, osg, oat, xs, modl, g2, w_out_b, *, tm, mod_row):
    n, d = xs.shape
    row = lambda width: pl.BlockSpec((tm, width), lambda i: (i, 0))
    return pl.pallas_call(
        _out_kernel,
        out_shape=(jax.ShapeDtypeStruct((n, d), F32), jax.ShapeDtypeStruct((n, d), BF16)),
        grid=(n // tm,),
        in_specs=[row(D_HY), row(D_SG), row(D_ATT), row(d), _mod_spec(mod_row),
                  pl.BlockSpec((1, d), lambda i: (0, 0)), _resident((D_MIX, d))],
        out_specs=(row(d), row(d)),
        compiler_params=_params(("parallel",)),
        name="out_proj",
    )(ohy, osg, oat, xs, modl, g2, w_out_b)


def _ffn_kernel(h_ref, hp_ref, hn_ref, nx_ref, mod_ref, wu_ref, cw_ref, cb_ref, wd_ref, o_ref,
                hc_scr, ga_scr, va_scr, gb_scr, vb_scr, *, tm, seq):
    d = D_MODEL
    half = d // 2
    nchunk = D_FF // FF_CHUNK
    rows_ext = tm + 2 * HALO
    start = (pl.program_id(0) * tm) % seq
    hp, hn = hp_ref[...], hn_ref[...]
    hc_scr[0:HALO, :] = jnp.where(start == 0, jnp.zeros_like(hp), hp)
    hc_scr[HALO:HALO + tm, :] = h_ref[...]
    hc_scr[HALO + tm:rows_ext, :] = jnp.where(start + tm == seq, jnp.zeros_like(hn), hn)

    def up(j, g_scr, v_scr):
        hc = hc_scr[...]
        g_scr[...] = _bdot(hc, wu_ref[j])
        v_scr[...] = _bdot(hc, wu_ref[nchunk + j])

    def conv(u_scr, j):
        u = u_scr[...]
        w = cw_ref[j]
        prev = pltpu.roll(u, 1, 0)[HALO:HALO + tm]
        nxt = pltpu.roll(u, rows_ext - 1, 0)[HALO:HALO + tm]
        return prev * w[0:1] + u[HALO:HALO + tm] * w[1:2] + nxt * w[2:3] + cb_ref[j]

    def activate(j, g_scr, v_scr):
        gate = conv(g_scr, j)
        val = conv(v_scr, nchunk + j)
        return (gate * jax.nn.sigmoid(gate) * val).astype(BF16)

    def down(j, act):
        o_ref[:, :half] += _bdot(act, wd_ref[j, :, :half])
        o_ref[:, half:] += _bdot(act, wd_ref[j, :, half:])

    o_ref[...] = jnp.zeros_like(o_ref)
    up(0, ga_scr, va_scr)

    def pair(i, carry):
        j = 2 * i
        act = activate(j, ga_scr, va_scr)
        up(j + 1, gb_scr, vb_scr)
        down(j, act)
        act = activate(j + 1, gb_scr, vb_scr)
        up(j + 2, ga_scr, va_scr)
        down(j + 1, act)
        return carry

    lax.fori_loop(0, (nchunk - 1) // 2, pair, 0)
    down(nchunk - 1, activate(nchunk - 1, ga_scr, va_scr))
    o_ref[...] = nx_ref[...] + mod_ref[0, :, 5 * d:6 * d] * o_ref[...]


def _ffn(h2, nx, modl, wu, cw, cb, wd, *, tm, seq, mod_row):
    n, d = nx.shape
    assert seq % tm == 0 and (D_FF // FF_CHUNK) % 2 == 1
    nh = n // HALO
    per = tm // HALO
    rows_ext = tm + 2 * HALO
    row = lambda width: pl.BlockSpec((tm, width), lambda i: (i, 0))
    return pl.pallas_call(
        functools.partial(_ffn_kernel, tm=tm, seq=seq),
        scratch_shapes=[pltpu.VMEM((rows_ext, d), BF16)] + [pltpu.VMEM((rows_ext, FF_CHUNK), F32)] * 4,
        out_shape=jax.ShapeDtypeStruct((n, d), F32),
        grid=(n // tm,),
        in_specs=[row(d),
                  pl.BlockSpec((HALO, d), lambda i: (jnp.maximum(i * per - 1, 0), 0)),
                  pl.BlockSpec((HALO, d), lambda i: (jnp.minimum((i + 1) * per, nh - 1), 0)),
                  row(d), _mod_spec(mod_row),
                  _resident(wu.shape), _resident(cw.shape), _resident(cb.shape), _resident(wd.shape)],
        out_specs=row(d),
        compiler_params=_params(("parallel",)),
        name="conv_ffn",
    )(h2, h2, h2, nx, modl, wu, cw, cb, wd)


def _pad_to(a, shape):
    return jnp.pad(a, [(0, t - s) for s, t in zip(a.shape, shape)])


def kernel(x, c, ctx, c_ctx, w_mod, b_mod, norm1_g, w_in, hy_conv_w, hy_conv_b, hy_w1, hy_b1, hy_w2, hy_b2,
           hy_w3, hy_freq, hy_bias, sg_ln_g, sg_ln_b, sg_w, sg_b, q_norm_g, k_norm_g, lam_q1, lam_k1, lam_q2,
           lam_k2, subln_g, w_out, norm2_g, ffn_w_up, ffn_conv_w, ffn_conv_b, ffn_w_down):
    bsz, seq, d = x.shape
    lc = ctx.shape[1]
    depth = w_mod.shape[0]
    assert d == D_MODEL and bsz + 1 <= MOD_ROWS and (bsz * lc) % 512 == 0 and seq % 512 == 0

    cc = _pad_to(jnp.concatenate([c, c_ctx[None, :]], axis=0), (MOD_ROWS, d))
    mod = _modulation(cc, w_mod, b_mod)
    ctx_row = bsz

    w1p = _pad_to(hy_w1, (depth, LANES, LANES))
    b1p = _pad_to(hy_b1, (depth, LANES)).reshape(depth, 1, LANES)
    frp = _pad_to(hy_freq, (depth, LANES)).reshape(depth, 1, LANES)
    w2p = _pad_to(hy_w2, (depth, LANES, LANES))
    b2p = _pad_to(hy_b2, (depth, LANES)).reshape(depth, 1, LANES)
    w3p = _pad_to(hy_w3, (depth, LANES, HY_ORDER * 2 * D_HY))
    dft = {}
    for s in (seq, lc):
        cmat, smat, _, _ = _dft_tables(s)
        dft[s] = (jnp.asarray(cmat).astype(BF16), jnp.asarray(smat).astype(BF16))
    ktab = _hyena_filters(seq, w1p, b1p, frp, w2p, b2p, w3p, *dft[seq])
    nctx = depth - 1
    ktab_c = _hyena_filters(lc, w1p[:nctx], b1p[:nctx], frp[:nctx], w2p[:nctx], b2p[:nctx], w3p[:nctx], *dft[lc])

    tm = 512
    nchunk = D_FF // FF_CHUNK
    xs = x.reshape(bsz * seq, d)
    cs = ctx.reshape(bsz * lc, d)
    lat_row = lambda i: (i * tm) // seq
    ctx_mod = lambda i: ctx_row

    for l in range(depth):
        last = l == depth - 1
        lam_init = 0.8 - 0.6 * math.exp(-0.3 * l)
        modl = mod[l].reshape(MOD_ROWS, 1, 6 * d)
        g1 = norm1_g[l].reshape(1, d)
        g2 = norm2_g[l].reshape(1, d)
        w_in_b = w_in[l].astype(BF16)
        w_out_b = w_out[l].astype(BF16)
        wu = ffn_w_up[l].astype(BF16).reshape(d, 2 * nchunk, FF_CHUNK).transpose(1, 0, 2)
        wd = ffn_w_down[l].astype(BF16).reshape(nchunk, FF_CHUNK, d)
        fcw = ffn_conv_w[l].reshape(3, 2 * nchunk, FF_CHUNK).transpose(1, 0, 2)
        fcb = ffn_conv_b[l].reshape(2 * nchunk, 1, FF_CHUNK)
        qn = jnp.tile(q_norm_g[l], LANES // QK_DIM).reshape(1, LANES)
        kn = jnp.tile(k_norm_g[l], LANES // QK_DIM).reshape(1, LANES)
        lng = sg_ln_g[l].reshape(1, D_SG)
        lnb = sg_ln_b[l].reshape(1, D_SG)
        wcat = sg_w[l].transpose(1, 0, 2).reshape(SG_CHUNK, SG_GROUPS * SG_CHUNK).astype(BF16)
        bstab = jnp.repeat(sg_b[l].T, D_SG // SG_GROUPS, axis=1)
        lamv = _pad_to(jnp.stack([lam_q1[l], lam_k1[l], lam_q2[l], lam_k2[l]]), (4, LANES))
        subg = subln_g[l].reshape(1, V_DIM)
        hcw = hy_conv_w[l]
        hcb = hy_conv_b[l].reshape(1, IN_A)
        sgu = (qn, kn, lng, lnb, wcat, bstab)

        pa, osg, q, k, v = _in_proj(xs, modl, g1, w_in_b, *sgu, seq=seq, tm=tm, use_rope=True, mod_row=lat_row)
        if last:
            kc, vc = _in_proj_kv(cs, modl, g1, w_in_b[:, IN_A + IN_B + D_QK:], kn, tm=tm, mod_row=ctx_mod)
        else:
            pa_c, osg_c, qc, kc, vc = _in_proj(cs, modl, g1, w_in_b, *sgu, seq=lc, tm=tm, use_rope=False,
                                               mod_row=ctx_mod)
        b3 = lambda a, t: a.reshape(bsz, t, a.shape[-1])
        kc3, vc3 = b3(kc, lc), b3(vc, lc)
        oat = _attention(lamv, b3(q, seq), [(b3(k, seq), b3(v, seq), ATT_KEY_PIECES), (kc3, vc3, 1)], subg,
                         tq=256, lam_init=lam_init)
        ohy = _hyena(b3(pa, seq), hcw, hcb, *dft[seq], ktab[l], hy_bias[l], rb=512)
        nx, h2 = _out_proj(ohy.reshape(-1, D_HY), osg, oat.reshape(-1, D_ATT), xs, modl, g2, w_out_b,
                           tm=tm, mod_row=lat_row)
        xs_next = _ffn(h2, nx, modl, wu, fcw, fcb, wd, tm=tm, seq=seq, mod_row=lat_row)

        if not last:
            oat_c = _attention(lamv, b3(qc, lc), [(kc3, vc3, 1)], subg, tq=lc, lam_init=lam_init)
            ohy_c = _hyena(b3(pa_c, lc), hcw, hcb, *dft[lc], ktab_c[l], hy_bias[l], rb=lc)
            nx_c, h2_c = _out_proj(ohy_c.reshape(-1, D_HY), osg_c, oat_c.reshape(-1, D_ATT), cs, modl, g2,
                                   w_out_b, tm=tm, mod_row=ctx_mod)
            cs = _ffn(h2_c, nx_c, modl, wu, fcw, fcb, wd, tm=min(tm, lc), seq=lc, mod_row=ctx_mod)
        xs = xs_next
    return xs.reshape(bsz, seq, d)
```

```python
import functools
import math

import jax
import jax.numpy as jnp
import numpy as np
from jax import lax
from jax.experimental import pallas as pl
from jax.experimental.pallas import tpu as pltpu

F32 = jnp.float32
BF16 = jnp.bfloat16
HIGHEST = lax.Precision.HIGHEST

D_MODEL = 1024
DEPTH = 2
GRID_W = 64
EPS = 1e-6
D_HY = 256
HY_ORDER = 2
HY_BANDS = 16
HY_EMB = 1 + 2 * HY_BANDS
HY_FFN = 64
HY_DECAY_TARGET = 1e-2
HY_FAST_DECAY = 0.3
HY_SLOW_DECAY = 1.5
SG_GROUPS = 4
D_SG = 256
SG_CHUNK = 128
ATT_HEADS = 4
QK_DIM = 64
V_DIM = 128
D_ATT = ATT_HEADS * V_DIM
ROPE_PAIRS = QK_DIM // 4
ROPE_BASE = 10000.0
D_MIX = D_HY + D_SG + D_ATT
IN_A = 3 * D_HY
IN_B = 2 * D_SG
D_QK = 2 * ATT_HEADS * QK_DIM
IN_C = 2 * D_QK + D_ATT
D_IN = IN_A + IN_B + IN_C
D_FF = 2816
D_VAUG = 2 * D_ATT
LOG2E = 1.4426950408889634

LANES = 128
BF16_SUBLANES = 16
VMEM_LIMIT_BYTES = 56 * 1024 * 1024

MOD_ROWS = 16
FF_CHUNK = 256
ATT_KEY_PIECES = 2
HALO = BF16_SUBLANES


def _params(sem, vmem=VMEM_LIMIT_BYTES):
    return pltpu.CompilerParams(dimension_semantics=sem, vmem_limit_bytes=vmem)


def _resident(shape):
    nd = len(shape)
    return pl.BlockSpec(shape, lambda *_: (0,) * nd, pipeline_mode=pl.Buffered(1))


@functools.lru_cache(maxsize=None)
def _dft_tables(seq):
    n2 = 2 * seq
    f = np.arange(seq, dtype=np.int64)
    m = ((2 * f[:, None] + 1) * (2 * f[None, :] + 1)) % (4 * n2)
    ang = (2.0 * np.pi / (4 * n2)) * m.astype(np.float64)
    half = np.pi * (2 * f + 1) / (2.0 * n2)
    return (np.cos(ang).astype(np.float32), np.sin(ang).astype(np.float32),
            np.cos(half).astype(np.float32)[:, None], np.sin(half).astype(np.float32)[:, None])


@functools.lru_cache(maxsize=None)
def _filter_tables(seq):
    t = np.linspace(0.0, 1.0, seq, dtype=np.float32)[:, None]
    t_r = np.arange(seq, dtype=np.float32)[:, None]
    bands = np.linspace(1e-4, HY_BANDS - 1, HY_BANDS, dtype=np.float32)[None, :]
    w = (2.0 * math.pi * t_r / seq).astype(np.float32)
    z = np.concatenate([t, np.cos(bands * w), -np.sin(bands * w)], axis=-1).astype(np.float32)
    zp = np.zeros((seq, LANES), np.float32)
    zp[:, :HY_EMB] = z
    min_decay = math.log(HY_DECAY_TARGET) / HY_SLOW_DECAY
    max_decay = math.log(HY_DECAY_TARGET) / HY_FAST_DECAY
    deltas = np.abs(np.linspace(min_decay, max_decay, D_HY, dtype=np.float32))
    decay = np.exp(-t * deltas[None, :]).astype(np.float32)
    return zp, decay


@functools.lru_cache(maxsize=None)
def _filter_tables_reversed(seq, lb):
    zp, decay = _filter_tables(seq)
    idx = lb - np.arange(lb)
    return np.ascontiguousarray(zp[idx]), np.ascontiguousarray(decay[idx])


@functools.lru_cache(maxsize=None)
def _rope_tables(seq):
    pos = np.arange(seq)
    row = (pos // GRID_W).astype(np.float32)
    col = (pos % GRID_W).astype(np.float32)
    inv = (ROPE_BASE ** (-np.arange(ROPE_PAIRS, dtype=np.float32) / ROPE_PAIRS)).astype(np.float32)
    lane = np.arange(LANES)
    axis = (lane % QK_DIM) // (2 * ROPE_PAIRS)
    half = (lane % (2 * ROPE_PAIRS)) // ROPE_PAIRS
    pair = lane % ROPE_PAIRS
    p = np.where(axis[None, :] == 0, row[:, None], col[:, None]).astype(np.float32)
    ang = (p * inv[pair][None, :]).astype(np.float32)
    cos, sin = np.cos(ang).astype(np.float32), np.sin(ang).astype(np.float32)
    sin_lo = np.where(half[None, :] == 0, -sin, 0.0).astype(np.float32)
    sin_hi = np.where(half[None, :] == 1, sin, 0.0).astype(np.float32)
    return cos, sin_lo, sin_hi


def _group_sum(x):
    lane = lax.broadcasted_iota(jnp.int32, (1, LANES), 1)
    lo = lane < QK_DIM
    s_lo = jnp.sum(jnp.where(lo, x, 0.0), axis=-1, keepdims=True)
    s_hi = jnp.sum(jnp.where(lo, 0.0, x), axis=-1, keepdims=True)
    return jnp.where(lo, s_lo, s_hi)


def _rmsnorm_rows(x, g):
    return x * lax.rsqrt(jnp.mean(x * x, axis=-1, keepdims=True) + EPS) * g


def _bdot(a, b):
    return jnp.dot(a, b, preferred_element_type=F32)


def _v_aug(v):
    ones = jnp.ones((v.shape[0], V_DIM), BF16)
    vb = v.astype(BF16)
    parts = []
    for h in range(ATT_HEADS):
        parts += [vb[:, h * V_DIM:(h + 1) * V_DIM], ones]
    return jnp.concatenate(parts, axis=-1)


def _mod_kernel(cc_ref, w_ref, b_ref, o_ref):
    cc = cc_ref[...]
    a = cc * jax.nn.sigmoid(cc)
    o_ref[0] = jnp.dot(a, w_ref[0], preferred_element_type=F32, precision=HIGHEST) + b_ref[0]


def _modulation(cc, w_mod, b_mod):
    depth, d, n = w_mod.shape
    tn = 768
    return pl.pallas_call(
        _mod_kernel,
        out_shape=jax.ShapeDtypeStruct((depth, MOD_ROWS, n), F32),
        grid=(depth, n // tn),
        in_specs=[pl.BlockSpec((MOD_ROWS, d), lambda l, j: (0, 0)),
                  pl.BlockSpec((1, d, tn), lambda l, j: (l, 0, j)),
                  pl.BlockSpec((1, 1, tn), lambda l, j: (l, 0, j))],
        out_specs=pl.BlockSpec((1, MOD_ROWS, tn), lambda l, j: (l, 0, j)),
        compiler_params=_params(("parallel", "parallel")),
        name="adaln_modulation",
    )(cc, w_mod, b_mod.reshape(depth, 1, n))


def _filter_kernel(z_ref, w1_ref, b1_ref, fr_ref, w2_ref, b2_ref, w3_ref, dec_ref, cm_ref, sm_ref,
                   ch_ref, sh_ref, o_ref, *, seq):
    fr = fr_ref[0]
    h = jnp.sin(fr * (jnp.dot(z_ref[...], w1_ref[0], preferred_element_type=F32, precision=HIGHEST) + b1_ref[0]))
    h = jnp.sin(fr * (jnp.dot(h, w2_ref[0], preferred_element_type=F32, precision=HIGHEST) + b2_ref[0]))
    h = jnp.dot(h, w3_ref[0], preferred_element_type=F32, precision=HIGHEST)
    dec = dec_ref[...]
    hf = h[:, :D_HY] * dec
    row = lax.broadcasted_iota(jnp.int32, (seq, 1), 0)
    hb = jnp.where(row > 0, h[:, D_HY:] * dec, 0.0)
    gp = (hf + hb).astype(BF16)
    gm = (hf - hb).astype(BF16)
    cm, sm = cm_ref[...], sm_ref[...]
    ch, sh = ch_ref[...], sh_ref[...]
    scale = 1.0 / seq
    o_ref[0, 0] = (_bdot(cm, gp) * ch + _bdot(sm, gp) * sh) * scale
    o_ref[0, 1] = (_bdot(cm, gm) * sh - _bdot(sm, gm) * ch) * scale


def _hyena_filters(seq, w1p, b1p, frp, w2p, b2p, w3, cm, sm):
    layers = w1p.shape[0]
    zp, decay = _filter_tables(seq)
    _, _, ch, sh = _dft_tables(seq)
    out = pl.pallas_call(
        functools.partial(_filter_kernel, seq=seq),
        out_shape=jax.ShapeDtypeStruct((layers * HY_ORDER, 2, seq, D_HY), F32),
        grid=(layers, HY_ORDER),
        in_specs=[_resident((seq, LANES)),
                  pl.BlockSpec((1, LANES, LANES), lambda l, o: (l, 0, 0)),
                  pl.BlockSpec((1, 1, LANES), lambda l, o: (l, 0, 0)),
                  pl.BlockSpec((1, 1, LANES), lambda l, o: (l, 0, 0)),
                  pl.BlockSpec((1, LANES, LANES), lambda l, o: (l, 0, 0)),
                  pl.BlockSpec((1, 1, LANES), lambda l, o: (l, 0, 0)),
                  pl.BlockSpec((1, LANES, 2 * D_HY), lambda l, o: (l, 0, o)),
                  _resident((seq, D_HY)), _resident((seq, seq)), _resident((seq, seq)),
                  _resident((seq, 1)), _resident((seq, 1))],
        out_specs=pl.BlockSpec((1, 2, seq, D_HY), lambda l, o: (l * HY_ORDER + o, 0, 0, 0)),
        compiler_params=_params(("parallel", "parallel")),
        name=f"hyena_filter_spectra_{seq}",
    )(jnp.asarray(zp), w1p, b1p, frp, w2p, b2p, w3, jnp.asarray(decay), cm, sm, jnp.asarray(ch), jnp.asarray(sh))
    return out.reshape(layers, 2 * HY_ORDER, seq, D_HY)


def _filter_blocked_kernel(z_ref, zr_ref, w1_ref, b1_ref, fr_ref, w2_ref, b2_ref, w3_ref, dec_ref, decr_ref,
                           cm_ref, sm_ref, ch_ref, sh_ref, o_ref, *, lb):
    fr = fr_ref[0]

    def mlp(z):
        h = jnp.sin(fr * (jnp.dot(z, w1_ref[0], preferred_element_type=F32, precision=HIGHEST) + b1_ref[0]))
        h = jnp.sin(fr * (jnp.dot(h, w2_ref[0], preferred_element_type=F32, precision=HIGHEST) + b2_ref[0]))
        return jnp.dot(h, w3_ref[0], preferred_element_type=F32, precision=HIGHEST)

    hn = mlp(z_ref[...])
    hr = mlp(zr_ref[...])
    dec, decr = dec_ref[...], decr_ref[...]
    hf, hb = hn[:, :D_HY] * dec, hn[:, D_HY:] * dec
    hfr, hbr = hr[:, :D_HY] * decr, hr[:, D_HY:] * decr
    taps = ((hf[:lb], hb[:lb]),
            (hf[lb:], hfr),
            (hbr, hb[lb:]))
    row = lax.broadcasted_iota(jnp.int32, (lb, 1), 0)
    cm, sm = cm_ref[...], sm_ref[...]
    ch, sh = ch_ref[...], sh_ref[...]
    scale = 1.0 / lb
    for d, (fwd, bwd) in enumerate(taps):
        bwd = jnp.where(row > 0, bwd, 0.0)
        gp = (fwd + bwd).astype(BF16)
        gm = (fwd - bwd).astype(BF16)
        o_ref[0, d, 0] = (_bdot(cm, gp) * ch + _bdot(sm, gp) * sh) * scale
        o_ref[0, d, 1] = (_bdot(cm, gm) * sh - _bdot(sm, gm) * ch) * scale


def _hyena_filters_blocked(seq, lb, w1p, b1p, frp, w2p, b2p, w3, cm, sm):
    assert seq == 2 * lb
    layers = w1p.shape[0]
    zp, decay = _filter_tables(seq)
    zr, decr = _filter_tables_reversed(seq, lb)
    _, _, ch, sh = _dft_tables(lb)
    per_layer = lambda l, o: (l, 0, 0)
    out = pl.pallas_call(
        functools.partial(_filter_blocked_kernel, lb=lb),
        out_shape=jax.ShapeDtypeStruct((layers * HY_ORDER, 3, 2, lb, D_HY), F32),
        grid=(layers, HY_ORDER),
        in_specs=[_resident((seq, LANES)), _resident((lb, LANES)),
                  pl.BlockSpec((1, LANES, LANES), per_layer), pl.BlockSpec((1, 1, LANES), per_layer),
                  pl.BlockSpec((1, 1, LANES), per_layer), pl.BlockSpec((1, LANES, LANES), per_layer),
                  pl.BlockSpec((1, 1, LANES), per_layer),
                  pl.BlockSpec((1, LANES, 2 * D_HY), lambda l, o: (l, 0, o)),
                  _resident((seq, D_HY)), _resident((lb, D_HY)), _resident((lb, lb)), _resident((lb, lb)),
                  _resident((lb, 1)), _resident((lb, 1))],
        out_specs=pl.BlockSpec((1, 3, 2, lb, D_HY), lambda l, o: (l * HY_ORDER + o, 0, 0, 0, 0)),
        compiler_params=_params(("parallel", "parallel")),
        name=f"hyena_block_filter_spectra_{seq}",
    )(jnp.asarray(zp), jnp.asarray(zr), w1p, b1p, frp, w2p, b2p, w3, jnp.asarray(decay), jnp.asarray(decr),
      cm, sm, jnp.asarray(ch), jnp.asarray(sh))
    return out.reshape(layers, HY_ORDER, 3, 2, lb, D_HY)


def _prenorm(x_ref, mod_ref, g_ref, shift_col, scale_col):
    d = D_MODEL
    shift = mod_ref[0, :, shift_col * d:(shift_col + 1) * d]
    scale = mod_ref[0, :, scale_col * d:(scale_col + 1) * d]
    return (_rmsnorm_rows(x_ref[...], g_ref[...]) * (1.0 + scale) + shift).astype(BF16)


def _qk_head_blocks(p, gain, rope):
    out = []
    for h in range(ATT_HEADS):
        xb = p[:, h * LANES:(h + 1) * LANES]
        xb = xb * lax.rsqrt(_group_sum(xb * xb) * (1.0 / QK_DIM) + EPS) * gain
        if rope is not None:
            cos, sin_lo, sin_hi = rope
            xb = (xb * cos + pltpu.roll(xb, LANES - ROPE_PAIRS, 1) * sin_lo
                  + pltpu.roll(xb, ROPE_PAIRS, 1) * sin_hi)
        out.append(xb)
    return jnp.concatenate(out, axis=-1)


def _in_kernel(*refs, tm, use_rope):
    if use_rope:
        (x_ref, mod_ref, g_ref, w_ref, cos_ref, slo_ref, shi_ref, qn_ref, kn_ref, lng_ref, lnb_ref,
         wcat_ref, bst_ref, pa_ref, osg_ref, q_ref, k_ref, v_ref) = refs
        rope = (cos_ref[...], slo_ref[...], shi_ref[...])
    else:
        (x_ref, mod_ref, g_ref, w_ref, qn_ref, kn_ref, lng_ref, lnb_ref,
         wcat_ref, bst_ref, pa_ref, osg_ref, q_ref, k_ref, v_ref) = refs
        rope = None
    hb = _prenorm(x_ref, mod_ref, g_ref, 0, 1)

    c0 = IN_A + IN_B
    pb = _bdot(hb, w_ref[:, IN_A:c0])
    pq = _bdot(hb, w_ref[:, c0:c0 + D_QK])
    pk = _bdot(hb, w_ref[:, c0 + D_QK:c0 + 2 * D_QK])

    pb = jax.nn.gelu(pb)
    u = pb[:, :D_SG]
    vn = []
    for j in range(D_SG // LANES):
        xb = pb[:, D_SG + j * LANES:D_SG + (j + 1) * LANES]
        xc = xb - _group_sum(xb) * (1.0 / QK_DIM)
        vn.append(xc * lax.rsqrt(_group_sum(xc * xc) * (1.0 / QK_DIM) + EPS))
    vn = (jnp.concatenate(vn, axis=-1) * lng_ref[...] + lnb_ref[...]).astype(BF16)
    lane = lax.broadcasted_iota(jnp.int32, (1, D_SG), 1)
    grp = lane // (D_SG // SG_GROUPS)
    zero = jnp.zeros((SG_CHUNK, D_SG), BF16)
    for ci in range(tm // SG_CHUNK):
        rows = slice(ci * SG_CHUNK, (ci + 1) * SG_CHUNK)
        vc = vn[rows]
        stacked = jnp.concatenate([jnp.where(grp == g, vc, zero) for g in range(SG_GROUPS)], axis=0)
        s = _bdot(wcat_ref[...], stacked) + bst_ref[...]
        osg_ref[rows, :] = (u[rows] * s).astype(BF16)

    q = _qk_head_blocks(pq, qn_ref[...], rope)
    q_ref[...] = (q * (QK_DIM ** -0.5 * LOG2E)).astype(BF16)
    pa_ref[...] = _bdot(hb, w_ref[:, :IN_A])
    k_ref[...] = _qk_head_blocks(pk, kn_ref[...], rope).astype(BF16)
    v_ref[...] = _v_aug(_bdot(hb, w_ref[:, c0 + 2 * D_QK:]))


def _in_kv_kernel(x_ref, mod_ref, g_ref, w_ref, kn_ref, k_ref, v_ref):
    hb = _prenorm(x_ref, mod_ref, g_ref, 0, 1)
    k = _qk_head_blocks(_bdot(hb, w_ref[:, :D_QK]), kn_ref[...], None)
    k_ref[...] = k.astype(BF16)
    v_ref[...] = _v_aug(_bdot(hb, w_ref[:, D_QK:]))


def _mod_spec(mod_row):
    return pl.BlockSpec((1, 1, 6 * D_MODEL), lambda i: (mod_row(i), 0, 0))


def _in_proj(xs, modl, g1, w_in_b, qn, kn, lng, lnb, wcat, bstab, *, seq, tm, use_rope, mod_row):
    n, d = xs.shape
    row = lambda width: pl.BlockSpec((tm, width), lambda i: (i, 0))
    vec = lambda width: pl.BlockSpec((1, width), lambda i: (0, 0))
    in_specs = [row(d), _mod_spec(mod_row), vec(d), _resident((d, D_IN))]
    args = [xs, modl, g1, w_in_b]
    if use_rope:
        nblk = seq // tm
        in_specs += [pl.BlockSpec((tm, LANES), lambda i: (i % nblk, 0))] * 3
        args += [jnp.asarray(t) for t in _rope_tables(seq)]
    in_specs += [vec(LANES), vec(LANES), vec(D_SG), vec(D_SG), _resident((SG_CHUNK, SG_GROUPS * SG_CHUNK)),
                 _resident((SG_CHUNK, D_SG))]
    args += [qn, kn, lng, lnb, wcat, bstab]
    return pl.pallas_call(
        functools.partial(_in_kernel, tm=tm, use_rope=use_rope),
        out_shape=(jax.ShapeDtypeStruct((n, IN_A), F32), jax.ShapeDtypeStruct((n, D_SG), BF16),
                   jax.ShapeDtypeStruct((n, D_QK), BF16), jax.ShapeDtypeStruct((n, D_QK), BF16),
                   jax.ShapeDtypeStruct((n, D_VAUG), BF16)),
        grid=(n // tm,),
        in_specs=in_specs,
        out_specs=(row(IN_A), row(D_SG), row(D_QK), row(D_QK), row(D_VAUG)),
        compiler_params=_params(("parallel",)),
        name="in_proj_rope" if use_rope else "in_proj",
    )(*args)


def _in_proj_kv(xs, modl, g1, w_kv_b, kn, *, tm, mod_row):
    n, d = xs.shape
    row = lambda width: pl.BlockSpec((tm, width), lambda i: (i, 0))
    vec = lambda width: pl.BlockSpec((1, width), lambda i: (0, 0))
    return pl.pallas_call(
        _in_kv_kernel,
        out_shape=(jax.ShapeDtypeStruct((n, D_QK), BF16), jax.ShapeDtypeStruct((n, D_VAUG), BF16)),
        grid=(n // tm,),
        in_specs=[row(d), _mod_spec(mod_row), vec(d), _resident((d, D_QK + D_ATT)), vec(LANES)],
        out_specs=(row(D_QK), row(D_VAUG)),
        compiler_params=_params(("parallel",)),
        name="in_proj_kv",
    )(xs, modl, g1, w_kv_b, kn)


def _att_kernel(*refs, segments, lam_init, tq):
    lam_ref, q_ref = refs[0], refs[1]
    kv_refs = refs[2:2 + 2 * segments]
    g_ref, o_ref = refs[2 + 2 * segments], refs[3 + 2 * segments]
    lv = lam_ref[...]
    lam = (jnp.exp(jnp.sum(lv[0:1] * lv[1:2], axis=-1, keepdims=True))
           - jnp.exp(jnp.sum(lv[2:3] * lv[3:4], axis=-1, keepdims=True)) + lam_init)
    lo = lax.broadcasted_iota(jnp.int32, (1, LANES), 1) < QK_DIM
    nt = (((1,), (1,)), ((), ()))

    def head_scores(h):
        cols = slice(h * LANES, (h + 1) * LANES)
        qh = q_ref[0, :, cols]
        zero = jnp.zeros_like(qh)
        return [[lax.dot_general(qm, kv_refs[2 * i][0, :, cols], nt, preferred_element_type=F32)
                 for i in range(segments)]
                for qm in (jnp.where(lo, qh, zero), jnp.where(lo, zero, qh))]

    nxt = head_scores(0)
    for h in range(ATT_HEADS):
        cols = slice(h * LANES, (h + 1) * LANES)
        scores, nxt = nxt, (head_scores(h + 1) if h + 1 < ATT_HEADS else None)
        o = []
        for s in scores:
            m = jnp.max(s[0], axis=-1, keepdims=True)
            for si in s[1:]:
                m = jnp.maximum(m, jnp.max(si, axis=-1, keepdims=True))
            acc = None
            for i in range(segments):
                p = jnp.exp2((s[i] - m).astype(BF16))
                a_i = _bdot(p, kv_refs[2 * i + 1][0, :, 2 * h * V_DIM:2 * (h + 1) * V_DIM])
                acc = a_i if acc is None else acc + a_i
            o.append(acc[:, :V_DIM] / acc[:, V_DIM:])
        out = o[0] - lam * o[1]
        o_ref[0, :, cols] = (_rmsnorm_rows(out, g_ref[...]) * (1.0 - lam_init)).astype(BF16)


def _attention(lamv, q, kvs, subg, *, tq, lam_init):
    b, t, _ = q.shape
    in_specs = [pl.BlockSpec((4, LANES), lambda i, j: (0, 0)),
                pl.BlockSpec((1, tq, D_QK), lambda i, j: (i, j, 0))]
    args = [lamv, q]
    for k, v, pieces in kvs:
        tk = k.shape[1] // pieces
        for piece in range(pieces):
            in_specs += [pl.BlockSpec((1, tk, D_QK), lambda i, j, piece=piece: (i, piece, 0)),
                         pl.BlockSpec((1, tk, D_VAUG), lambda i, j, piece=piece: (i, piece, 0))]
            args += [k, v]
    in_specs.append(pl.BlockSpec((1, V_DIM), lambda i, j: (0, 0)))
    args.append(subg)
    nseg = (len(in_specs) - 3) // 2
    return pl.pallas_call(
        functools.partial(_att_kernel, segments=nseg, lam_init=lam_init, tq=tq),
        out_shape=jax.ShapeDtypeStruct((b, t, D_ATT), BF16),
        grid=(b, t // tq),
        in_specs=in_specs,
        out_specs=pl.BlockSpec((1, tq, D_ATT), lambda i, j: (i, j, 0)),
        compiler_params=_params(("parallel", "parallel")),
        name=f"diff_attention_{nseg}seg",
    )(*args)


def _hyena_kernel(pa_ref, cw_ref, cb_ref, cm_ref, sm_ref, kt_ref, bias_ref, o_ref,
                  p_scr, z_scr, zb_scr, pb_scr, qb_scr, *, seq, rb):
    row = lax.broadcasted_iota(jnp.int32, (seq, 1), 0)
    for j in range(IN_A // D_HY):
        cols = slice(j * D_HY, (j + 1) * D_HY)
        pa = pa_ref[0, :, cols]
        prev = jnp.where(row > 0, pltpu.roll(pa, 1, 0), 0.0)
        nxt = jnp.where(row < seq - 1, pltpu.roll(pa, seq - 1, 0), 0.0)
        p_scr[:, cols] = (prev * cw_ref[0:1, cols] + pa * cw_ref[1:2, cols] + nxt * cw_ref[2:3, cols]
                          + cb_ref[:, cols])

    z_scr[...] = p_scr[:, :D_HY]
    for o in range(HY_ORDER):
        zb_scr[...] = z_scr[...].astype(BF16)
        gate_cols = slice((o + 1) * D_HY, (o + 2) * D_HY)

        def spectrum(i, carry):
            rows = pl.ds(pl.multiple_of(i * rb, rb), rb)
            a = _bdot(cm_ref[rows, :], zb_scr[...])
            b = _bdot(sm_ref[rows, :], zb_scr[...])
            kre, kim = kt_ref[2 * o, rows, :], kt_ref[2 * o + 1, rows, :]
            pb_scr[rows, :] = (a * kre + b * kim).astype(BF16)
            qb_scr[rows, :] = (b * kre - a * kim).astype(BF16)
            return carry

        lax.fori_loop(0, seq // rb, spectrum, 0)

        def synth(i, carry):
            rows = pl.ds(pl.multiple_of(i * rb, rb), rb)
            y = _bdot(cm_ref[rows, :], pb_scr[...]) + _bdot(sm_ref[rows, :], qb_scr[...])
            z_scr[rows, :] = p_scr[rows, gate_cols] * (y + bias_ref[o:o + 1, :] * z_scr[rows, :])
            return carry

        lax.fori_loop(0, seq // rb, synth, 0)
    o_ref[0] = z_scr[...].astype(BF16)


def _hyena(pa, cw, cb, cm, sm, ktab, bias, *, rb):
    b, seq, _ = pa.shape
    return pl.pallas_call(
        functools.partial(_hyena_kernel, seq=seq, rb=rb),
        out_shape=jax.ShapeDtypeStruct((b, seq, D_HY), BF16),
        grid=(b,),
        in_specs=[pl.BlockSpec((1, seq, IN_A), lambda i: (i, 0, 0), pipeline_mode=pl.Buffered(1)),
                  _resident((3, IN_A)), _resident((1, IN_A)), _resident((seq, seq)), _resident((seq, seq)),
                  _resident((2 * HY_ORDER, seq, D_HY)), _resident((HY_ORDER, D_HY))],
        out_specs=pl.BlockSpec((1, seq, D_HY), lambda i: (i, 0, 0)),
        scratch_shapes=[pltpu.VMEM((seq, IN_A), F32), pltpu.VMEM((seq, D_HY), F32),
                        pltpu.VMEM((seq, D_HY), BF16), pltpu.VMEM((seq, D_HY), BF16),
                        pltpu.VMEM((seq, D_HY), BF16)],
        compiler_params=_params(("parallel",)),
        name=f"hyena_mixer_{seq}",
    )(pa, cw, cb, cm, sm, ktab, bias)


def _hyena_blocked_kernel(pa_ref, cw_ref, cb_ref, cm_ref, sm_ref, kt_ref, bias_ref, o_ref,
                          p_scr, z_scr, zb_scr, pb_scr, qb_scr, *, seq, lb, rb):
    row = lax.broadcasted_iota(jnp.int32, (seq, 1), 0)
    for j in range(IN_A // D_HY):
        cols = slice(j * D_HY, (j + 1) * D_HY)
        pa = pa_ref[0, :, cols]
        prev = jnp.where(row > 0, pltpu.roll(pa, 1, 0), 0.0)
        nxt = jnp.where(row < seq - 1, pltpu.roll(pa, seq - 1, 0), 0.0)
        p_scr[:, cols] = (prev * cw_ref[0:1, cols] + pa * cw_ref[1:2, cols] + nxt * cw_ref[2:3, cols]
                          + cb_ref[:, cols])

    per_block = lb // rb
    z_scr[...] = p_scr[:, :D_HY]
    for o in range(HY_ORDER):
        zb_scr[...] = z_scr[...].astype(BF16)
        gate_cols = slice((o + 1) * D_HY, (o + 2) * D_HY)

        def spectrum(i, carry):
            rows = pl.ds(pl.multiple_of(i * rb, rb), rb)
            cmr, smr = cm_ref[rows, :], sm_ref[rows, :]
            a0, b0 = _bdot(cmr, zb_scr[0:lb, :]), _bdot(smr, zb_scr[0:lb, :])
            a1, b1 = _bdot(cmr, zb_scr[lb:seq, :]), _bdot(smr, zb_scr[lb:seq, :])
            k0r, k0i = kt_ref[o, 0, 0, rows, :], kt_ref[o, 0, 1, rows, :]
            kpr, kpi = kt_ref[o, 1, 0, rows, :], kt_ref[o, 1, 1, rows, :]
            kmr, kmi = kt_ref[o, 2, 0, rows, :], kt_ref[o, 2, 1, rows, :]
            pb_scr[0, rows, :] = (a0 * k0r + b0 * k0i + a1 * kmr + b1 * kmi).astype(BF16)
            qb_scr[0, rows, :] = (b0 * k0r - a0 * k0i + b1 * kmr - a1 * kmi).astype(BF16)
            pb_scr[1, rows, :] = (a1 * k0r + b1 * k0i + a0 * kpr + b0 * kpi).astype(BF16)
            qb_scr[1, rows, :] = (b1 * k0r - a1 * k0i + b0 * kpr - a0 * kpi).astype(BF16)
            return carry

        lax.fori_loop(0, per_block, spectrum, 0)

        def synth(i, carry):
            blk = i // per_block
            rows_in = pl.ds(pl.multiple_of((i % per_block) * rb, rb), rb)
            rows_out = pl.ds(pl.multiple_of(i * rb, rb), rb)
            y = _bdot(cm_ref[rows_in, :], pb_scr[blk]) + _bdot(sm_ref[rows_in, :], qb_scr[blk])
            z_scr[rows_out, :] = p_scr[rows_out, gate_cols] * (y + bias_ref[o:o + 1, :] * z_scr[rows_out, :])
            return carry

        lax.fori_loop(0, seq // rb, synth, 0)
    o_ref[0] = z_scr[...].astype(BF16)


def _hyena_blocked(pa, cw, cb, cm, sm, ktab, bias, *, lb, rb):
    b, seq, _ = pa.shape
    return pl.pallas_call(
        functools.partial(_hyena_blocked_kernel, seq=seq, lb=lb, rb=rb),
        out_shape=jax.ShapeDtypeStruct((b, seq, D_HY), BF16),
        grid=(b,),
        in_specs=[pl.BlockSpec((1, seq, IN_A), lambda i: (i, 0, 0), pipeline_mode=pl.Buffered(1)),
                  _resident((3, IN_A)), _resident((1, IN_A)), _resident((lb, lb)), _resident((lb, lb)),
                  _resident((HY_ORDER, 3, 2, lb, D_HY)), _resident((HY_ORDER, D_HY))],
        out_specs=pl.BlockSpec((1, seq, D_HY), lambda i: (i, 0, 0)),
        scratch_shapes=[pltpu.VMEM((seq, IN_A), F32), pltpu.VMEM((seq, D_HY), F32),
                        pltpu.VMEM((seq, D_HY), BF16), pltpu.VMEM((2, lb, D_HY), BF16),
                        pltpu.VMEM((2, lb, D_HY), BF16)],
        compiler_params=_params(("parallel",)),
        name=f"hyena_block_mixer_{seq}",
    )(pa, cw, cb, cm, sm, ktab, bias)


def _out_kernel(ohy_ref, osg_ref, oat_ref, x_ref, mod_ref, g2_ref, w_ref, nx_ref, h2_ref):
    d = D_MODEL
    mix = (_bdot(ohy_ref[...], w_ref[:D_HY]) + _bdot(osg_ref[...], w_ref[D_HY:D_HY + D_SG])
           + _bdot(oat_ref[...], w_ref[D_HY + D_SG:]))
    nx = x_ref[...] + mod_ref[0, :, 2 * d:3 * d] * mix
    nx_ref[...] = nx
    h2 = _rmsnorm_rows(nx, g2_ref[...]) * (1.0 + mod_ref[0, :, 4 * d:5 * d]) + mod_ref[0, :, 3 * d:4 * d]
    h2_ref[...] = h2.astype(BF16)


def _out_proj(ohy, osg, oat, xs, modl, g2, w_out_b, *, tm, mod_row):
    n, d = xs.shape
    row = lambda width: pl.BlockSpec((tm, width), lambda i: (i, 0))
    return pl.pallas_call(
        _out_kernel,
        out_shape=(jax.ShapeDtypeStruct((n, d), F32), jax.ShapeDtypeStruct((n, d), BF16)),
        grid=(n // tm,),
        in_specs=[row(D_HY), row(D_SG), row(D_ATT), row(d), _mod_spec(mod_row),
                  pl.BlockSpec((1, d), lambda i: (0, 0)), _resident((D_MIX, d))],
        out_specs=(row(d), row(d)),
        compiler_params=_params(("parallel",)),
        name="out_proj",
    )(ohy, osg, oat, xs, modl, g2, w_out_b)


def _ffn_kernel(h_ref, hp_ref, hn_ref, nx_ref, mod_ref, wu_ref, cw_ref, cb_ref, wd_ref, o_ref,
                hc_scr, ga_scr, va_scr, gb_scr, vb_scr, *, tm, seq):
    d = D_MODEL
    half = d // 2
    nchunk = D_FF // FF_CHUNK
    rows_ext = tm + 2 * HALO
    start = (pl.program_id(0) * tm) % seq
    hp, hn = hp_ref[...], hn_ref[...]
    hc_scr[0:HALO, :] = jnp.where(start == 0, jnp.zeros_like(hp), hp)
    hc_scr[HALO:HALO + tm, :] = h_ref[...]
    hc_scr[HALO + tm:rows_ext, :] = jnp.where(start + tm == seq, jnp.zeros_like(hn), hn)

    def up(j, g_scr, v_scr):
        hc = hc_scr[...]
        g_scr[...] = _bdot(hc, wu_ref[j])
        v_scr[...] = _bdot(hc, wu_ref[nchunk + j])

    def conv(u_scr, j):
        u = u_scr[...]
        w = cw_ref[j]
        prev = pltpu.roll(u, 1, 0)[HALO:HALO + tm]
        nxt = pltpu.roll(u, rows_ext - 1, 0)[HALO:HALO + tm]
        return prev * w[0:1] + u[HALO:HALO + tm] * w[1:2] + nxt * w[2:3] + cb_ref[j]

    def activate(j, g_scr, v_scr):
        gate = conv(g_scr, j)
        val = conv(v_scr, nchunk + j)
        return (gate * jax.nn.sigmoid(gate) * val).astype(BF16)

    def down(j, act):
        o_ref[:, :half] += _bdot(act, wd_ref[j, :, :half])
        o_ref[:, half:] += _bdot(act, wd_ref[j, :, half:])

    o_ref[...] = jnp.zeros_like(o_ref)
    up(0, ga_scr, va_scr)

    def pair(i, carry):
        j = 2 * i
        act = activate(j, ga_scr, va_scr)
        up(j + 1, gb_scr, vb_scr)
        down(j, act)
        act = activate(j + 1, gb_scr, vb_scr)
        up(j + 2, ga_scr, va_scr)
        down(j + 1, act)
        return carry

    lax.fori_loop(0, (nchunk - 1) // 2, pair, 0)
    down(nchunk - 1, activate(nchunk - 1, ga_scr, va_scr))
    o_ref[...] = nx_ref[...] + mod_ref[0, :, 5 * d:6 * d] * o_ref[...]


def _ffn(h2, nx, modl, wu, cw, cb, wd, *, tm, seq, mod_row):
    n, d = nx.shape
    assert seq % tm == 0 and (D_FF // FF_CHUNK) % 2 == 1
    nh = n // HALO
    per = tm // HALO
    rows_ext = tm + 2 * HALO
    row = lambda width: pl.BlockSpec((tm, width), lambda i: (i, 0))
    return pl.pallas_call(
        functools.partial(_ffn_kernel, tm=tm, seq=seq),
        scratch_shapes=[pltpu.VMEM((rows_ext, d), BF16)] + [pltpu.VMEM((rows_ext, FF_CHUNK), F32)] * 4,
        out_shape=jax.ShapeDtypeStruct((n, d), F32),
        grid=(n // tm,),
        in_specs=[row(d),
                  pl.BlockSpec((HALO, d), lambda i: (jnp.maximum(i * per - 1, 0), 0)),
                  pl.BlockSpec((HALO, d), lambda i: (jnp.minimum((i + 1) * per, nh - 1), 0)),
                  row(d), _mod_spec(mod_row),
                  _resident(wu.shape), _resident(cw.shape), _resident(cb.shape), _resident(wd.shape)],
        out_specs=row(d),
        compiler_params=_params(("parallel",)),
        name="conv_ffn",
    )(h2, h2, h2, nx, modl, wu, cw, cb, wd)


def _pad_to(a, shape):
    return jnp.pad(a, [(0, t - s) for s, t in zip(a.shape, shape)])


def kernel(x, c, ctx, c_ctx, w_mod, b_mod, norm1_g, w_in, hy_conv_w, hy_conv_b, hy_w1, hy_b1, hy_w2, hy_b2,
           hy_w3, hy_freq, hy_bias, sg_ln_g, sg_ln_b, sg_w, sg_b, q_norm_g, k_norm_g, lam_q1, lam_k1, lam_q2,
           lam_k2, subln_g, w_out, norm2_g, ffn_w_up, ffn_conv_w, ffn_conv_b, ffn_w_down):
    bsz, seq, d = x.shape
    lc = ctx.shape[1]
    depth = w_mod.shape[0]
    assert d == D_MODEL and bsz + 1 <= MOD_ROWS and (bsz * lc) % 512 == 0 and seq % 512 == 0

    cc = _pad_to(jnp.concatenate([c, c_ctx[None, :]], axis=0), (MOD_ROWS, d))
    mod = _modulation(cc, w_mod, b_mod)
    ctx_row = bsz

    w1p = _pad_to(hy_w1, (depth, LANES, LANES))
    b1p = _pad_to(hy_b1, (depth, LANES)).reshape(depth, 1, LANES)
    frp = _pad_to(hy_freq, (depth, LANES)).reshape(depth, 1, LANES)
    w2p = _pad_to(hy_w2, (depth, LANES, LANES))
    b2p = _pad_to(hy_b2, (depth, LANES)).reshape(depth, 1, LANES)
    w3p = _pad_to(hy_w3, (depth, LANES, HY_ORDER * 2 * D_HY))
    lb = seq // 2
    dft = {}
    for s in (lb, lc):
        cmat, smat, _, _ = _dft_tables(s)
        dft[s] = (jnp.asarray(cmat).astype(BF16), jnp.asarray(smat).astype(BF16))
    ktab = _hyena_filters_blocked(seq, lb, w1p, b1p, frp, w2p, b2p, w3p, *dft[lb])
    nctx = depth - 1
    ktab_c = _hyena_filters(lc, w1p[:nctx], b1p[:nctx], frp[:nctx], w2p[:nctx], b2p[:nctx], w3p[:nctx], *dft[lc])

    tm = 512
    nchunk = D_FF // FF_CHUNK
    xs = x.reshape(bsz * seq, d)
    cs = ctx.reshape(bsz * lc, d)
    lat_row = lambda i: (i * tm) // seq
    ctx_mod = lambda i: ctx_row

    for l in range(depth):
        last = l == depth - 1
        lam_init = 0.8 - 0.6 * math.exp(-0.3 * l)
        modl = mod[l].reshape(MOD_ROWS, 1, 6 * d)
        g1 = norm1_g[l].reshape(1, d)
        g2 = norm2_g[l].reshape(1, d)
        w_in_b = w_in[l].astype(BF16)
        w_out_b = w_out[l].astype(BF16)
        wu = ffn_w_up[l].astype(BF16).reshape(d, 2 * nchunk, FF_CHUNK).transpose(1, 0, 2)
        wd = ffn_w_down[l].astype(BF16).reshape(nchunk, FF_CHUNK, d)
        fcw = ffn_conv_w[l].reshape(3, 2 * nchunk, FF_CHUNK).transpose(1, 0, 2)
        fcb = ffn_conv_b[l].reshape(2 * nchunk, 1, FF_CHUNK)
        qn = jnp.tile(q_norm_g[l], LANES // QK_DIM).reshape(1, LANES)
        kn = jnp.tile(k_norm_g[l], LANES // QK_DIM).reshape(1, LANES)
        lng = sg_ln_g[l].reshape(1, D_SG)
        lnb = sg_ln_b[l].reshape(1, D_SG)
        wcat = sg_w[l].transpose(1, 0, 2).reshape(SG_CHUNK, SG_GROUPS * SG_CHUNK).astype(BF16)
        bstab = jnp.repeat(sg_b[l].T, D_SG // SG_GROUPS, axis=1)
        lamv = _pad_to(jnp.stack([lam_q1[l], lam_k1[l], lam_q2[l], lam_k2[l]]), (4, LANES))
        subg = subln_g[l].reshape(1, V_DIM)
        hcw = hy_conv_w[l]
        hcb = hy_conv_b[l].reshape(1, IN_A)
        sgu = (qn, kn, lng, lnb, wcat, bstab)

        pa, osg, q, k, v = _in_proj(xs, modl, g1, w_in_b, *sgu, seq=seq, tm=tm, use_rope=True, mod_row=lat_row)
        if last:
            kc, vc = _in_proj_kv(cs, modl, g1, w_in_b[:, IN_A + IN_B + D_QK:], kn, tm=tm, mod_row=ctx_mod)
        else:
            pa_c, osg_c, qc, kc, vc = _in_proj(cs, modl, g1, w_in_b, *sgu, seq=lc, tm=tm, use_rope=False,
                                               mod_row=ctx_mod)
        b3 = lambda a, t: a.reshape(bsz, t, a.shape[-1])
        kc3, vc3 = b3(kc, lc), b3(vc, lc)
        oat = _attention(lamv, b3(q, seq), [(b3(k, seq), b3(v, seq), ATT_KEY_PIECES), (kc3, vc3, 1)], subg,
                         tq=256, lam_init=lam_init)
        ohy = _hyena_blocked(b3(pa, seq), hcw, hcb, *dft[lb], ktab[l], hy_bias[l], lb=lb, rb=512)
        nx, h2 = _out_proj(ohy.reshape(-1, D_HY), osg, oat.reshape(-1, D_ATT), xs, modl, g2, w_out_b,
                           tm=tm, mod_row=lat_row)
        xs_next = _ffn(h2, nx, modl, wu, fcw, fcb, wd, tm=tm, seq=seq, mod_row=lat_row)

        if not last:
            oat_c = _attention(lamv, b3(qc, lc), [(kc3, vc3, 1)], subg, tq=lc, lam_init=lam_init)
            ohy_c = _hyena(b3(pa_c, lc), hcw, hcb, *dft[lc], ktab_c[l], hy_bias[l], rb=lc)
            nx_c, h2_c = _out_proj(ohy_c.reshape(-1, D_HY), osg_c, oat_c.reshape(-1, D_ATT), cs, modl, g2,
                                   w_out_b, tm=tm, mod_row=ctx_mod)
            cs = _ffn(h2_c, nx_c, modl, wu, fcw, fcb, wd, tm=min(tm, lc), seq=lc, mod_row=ctx_mod)
        xs = xs_next
    return xs.reshape(bsz, seq, d)
```

```python
import functools
import math

import jax
import jax.numpy as jnp
import numpy as np
from jax import lax
from jax.experimental import pallas as pl
from jax.experimental.pallas import tpu as pltpu

F32 = jnp.float32
BF16 = jnp.bfloat16
HIGHEST = lax.Precision.HIGHEST

D_MODEL = 1024
DEPTH = 2
GRID_W = 64
EPS = 1e-6
D_HY = 256
HY_ORDER = 2
HY_BANDS = 16
HY_EMB = 1 + 2 * HY_BANDS
HY_FFN = 64
HY_DECAY_TARGET = 1e-2
HY_FAST_DECAY = 0.3
HY_SLOW_DECAY = 1.5
SG_GROUPS = 4
D_SG = 256
SG_CHUNK = 128
ATT_HEADS = 4
QK_DIM = 64
V_DIM = 128
D_ATT = ATT_HEADS * V_DIM
ROPE_PAIRS = QK_DIM // 4
ROPE_BASE = 10000.0
D_MIX = D_HY + D_SG + D_ATT
IN_A = 3 * D_HY
IN_B = 2 * D_SG
D_QK = 2 * ATT_HEADS * QK_DIM
IN_C = 2 * D_QK + D_ATT
D_IN = IN_A + IN_B + IN_C
D_FF = 2816
D_VAUG = 2 * D_ATT
LOG2E = 1.4426950408889634

LANES = 128
BF16_SUBLANES = 16
VMEM_LIMIT_BYTES = 56 * 1024 * 1024

MOD_ROWS = 16
FF_CHUNK = 256
ATT_KEY_PIECES = 1
HALO = BF16_SUBLANES


def _params(sem, vmem=VMEM_LIMIT_BYTES):
    return pltpu.CompilerParams(dimension_semantics=sem, vmem_limit_bytes=vmem)


def _resident(shape):
    nd = len(shape)
    return pl.BlockSpec(shape, lambda *_: (0,) * nd, pipeline_mode=pl.Buffered(1))


@functools.lru_cache(maxsize=None)
def _dft_tables(seq):
    n2 = 2 * seq
    f = np.arange(seq, dtype=np.int64)
    m = ((2 * f[:, None] + 1) * (2 * f[None, :] + 1)) % (4 * n2)
    ang = (2.0 * np.pi / (4 * n2)) * m.astype(np.float64)
    half = np.pi * (2 * f + 1) / (2.0 * n2)
    return (np.cos(ang).astype(np.float32), np.sin(ang).astype(np.float32),
            np.cos(half).astype(np.float32)[:, None], np.sin(half).astype(np.float32)[:, None])


@functools.lru_cache(maxsize=None)
def _filter_tables(seq):
    t = np.linspace(0.0, 1.0, seq, dtype=np.float32)[:, None]
    t_r = np.arange(seq, dtype=np.float32)[:, None]
    bands = np.linspace(1e-4, HY_BANDS - 1, HY_BANDS, dtype=np.float32)[None, :]
    w = (2.0 * math.pi * t_r / seq).astype(np.float32)
    z = np.concatenate([t, np.cos(bands * w), -np.sin(bands * w)], axis=-1).astype(np.float32)
    zp = np.zeros((seq, LANES), np.float32)
    zp[:, :HY_EMB] = z
    min_decay = math.log(HY_DECAY_TARGET) / HY_SLOW_DECAY
    max_decay = math.log(HY_DECAY_TARGET) / HY_FAST_DECAY
    deltas = np.abs(np.linspace(min_decay, max_decay, D_HY, dtype=np.float32))
    decay = np.exp(-t * deltas[None, :]).astype(np.float32)
    return zp, decay


@functools.lru_cache(maxsize=None)
def _filter_tables_reversed(seq, lb):
    zp, decay = _filter_tables(seq)
    idx = lb - np.arange(lb)
    return np.ascontiguousarray(zp[idx]), np.ascontiguousarray(decay[idx])


@functools.lru_cache(maxsize=None)
def _rope_tables(seq):
    pos = np.arange(seq)
    row = (pos // GRID_W).astype(np.float32)
    col = (pos % GRID_W).astype(np.float32)
    inv = (ROPE_BASE ** (-np.arange(ROPE_PAIRS, dtype=np.float32) / ROPE_PAIRS)).astype(np.float32)
    lane = np.arange(LANES)
    axis = (lane % QK_DIM) // (2 * ROPE_PAIRS)
    half = (lane % (2 * ROPE_PAIRS)) // ROPE_PAIRS
    pair = lane % ROPE_PAIRS
    p = np.where(axis[None, :] == 0, row[:, None], col[:, None]).astype(np.float32)
    ang = (p * inv[pair][None, :]).astype(np.float32)
    cos, sin = np.cos(ang).astype(np.float32), np.sin(ang).astype(np.float32)
    sin_lo = np.where(half[None, :] == 0, -sin, 0.0).astype(np.float32)
    sin_hi = np.where(half[None, :] == 1, sin, 0.0).astype(np.float32)
    return cos, sin_lo, sin_hi


def _group_sum(x):
    lane = lax.broadcasted_iota(jnp.int32, (1, LANES), 1)
    lo = lane < QK_DIM
    s_lo = jnp.sum(jnp.where(lo, x, 0.0), axis=-1, keepdims=True)
    s_hi = jnp.sum(jnp.where(lo, 0.0, x), axis=-1, keepdims=True)
    return jnp.where(lo, s_lo, s_hi)


def _rmsnorm_rows(x, g):
    return x * lax.rsqrt(jnp.mean(x * x, axis=-1, keepdims=True) + EPS) * g


def _bdot(a, b):
    return jnp.dot(a, b, preferred_element_type=F32)


def _v_aug(v):
    ones = jnp.ones((v.shape[0], V_DIM), BF16)
    vb = v.astype(BF16)
    parts = []
    for h in range(ATT_HEADS):
        parts += [vb[:, h * V_DIM:(h + 1) * V_DIM], ones]
    return jnp.concatenate(parts, axis=-1)


def _mod_kernel(cc_ref, w_ref, b_ref, o_ref):
    cc = cc_ref[...]
    a = cc * jax.nn.sigmoid(cc)
    o_ref[0] = jnp.dot(a, w_ref[0], preferred_element_type=F32, precision=HIGHEST) + b_ref[0]


def _modulation(cc, w_mod, b_mod):
    depth, d, n = w_mod.shape
    tn = 768
    return pl.pallas_call(
        _mod_kernel,
        out_shape=jax.ShapeDtypeStruct((depth, MOD_ROWS, n), F32),
        grid=(depth, n // tn),
        in_specs=[pl.BlockSpec((MOD_ROWS, d), lambda l, j: (0, 0)),
                  pl.BlockSpec((1, d, tn), lambda l, j: (l, 0, j)),
                  pl.BlockSpec((1, 1, tn), lambda l, j: (l, 0, j))],
        out_specs=pl.BlockSpec((1, MOD_ROWS, tn), lambda l, j: (l, 0, j)),
        compiler_params=_params(("parallel", "parallel")),
        name="adaln_modulation",
    )(cc, w_mod, b_mod.reshape(depth, 1, n))


def _filter_kernel(z_ref, w1_ref, b1_ref, fr_ref, w2_ref, b2_ref, w3_ref, dec_ref, cm_ref, sm_ref,
                   ch_ref, sh_ref, o_ref, *, seq):
    fr = fr_ref[0]
    h = jnp.sin(fr * (jnp.dot(z_ref[...], w1_ref[0], preferred_element_type=F32, precision=HIGHEST) + b1_ref[0]))
    h = jnp.sin(fr * (jnp.dot(h, w2_ref[0], preferred_element_type=F32, precision=HIGHEST) + b2_ref[0]))
    h = jnp.dot(h, w3_ref[0], preferred_element_type=F32, precision=HIGHEST)
    dec = dec_ref[...]
    hf = h[:, :D_HY] * dec
    row = lax.broadcasted_iota(jnp.int32, (seq, 1), 0)
    hb = jnp.where(row > 0, h[:, D_HY:] * dec, 0.0)
    gp = (hf + hb).astype(BF16)
    gm = (hf - hb).astype(BF16)
    cm, sm = cm_ref[...], sm_ref[...]
    ch, sh = ch_ref[...], sh_ref[...]
    scale = 1.0 / seq
    o_ref[0, 0] = (_bdot(cm, gp) * ch + _bdot(sm, gp) * sh) * scale
    o_ref[0, 1] = (_bdot(cm, gm) * sh - _bdot(sm, gm) * ch) * scale


def _hyena_filters(seq, w1p, b1p, frp, w2p, b2p, w3, cm, sm):
    layers = w1p.shape[0]
    zp, decay = _filter_tables(seq)
    _, _, ch, sh = _dft_tables(seq)
    out = pl.pallas_call(
        functools.partial(_filter_kernel, seq=seq),
        out_shape=jax.ShapeDtypeStruct((layers * HY_ORDER, 2, seq, D_HY), F32),
        grid=(layers, HY_ORDER),
        in_specs=[_resident((seq, LANES)),
                  pl.BlockSpec((1, LANES, LANES), lambda l, o: (l, 0, 0)),
                  pl.BlockSpec((1, 1, LANES), lambda l, o: (l, 0, 0)),
                  pl.BlockSpec((1, 1, LANES), lambda l, o: (l, 0, 0)),
                  pl.BlockSpec((1, LANES, LANES), lambda l, o: (l, 0, 0)),
                  pl.BlockSpec((1, 1, LANES), lambda l, o: (l, 0, 0)),
                  pl.BlockSpec((1, LANES, 2 * D_HY), lambda l, o: (l, 0, o)),
                  _resident((seq, D_HY)), _resident((seq, seq)), _resident((seq, seq)),
                  _resident((seq, 1)), _resident((seq, 1))],
        out_specs=pl.BlockSpec((1, 2, seq, D_HY), lambda l, o: (l * HY_ORDER + o, 0, 0, 0)),
        compiler_params=_params(("parallel", "parallel")),
        name=f"hyena_filter_spectra_{seq}",
    )(jnp.asarray(zp), w1p, b1p, frp, w2p, b2p, w3, jnp.asarray(decay), cm, sm, jnp.asarray(ch), jnp.asarray(sh))
    return out.reshape(layers, 2 * HY_ORDER, seq, D_HY)


def _filter_blocked_kernel(z_ref, zr_ref, w1_ref, b1_ref, fr_ref, w2_ref, b2_ref, w3_ref, dec_ref, decr_ref,
                           cm_ref, sm_ref, ch_ref, sh_ref, o_ref, hid_scr, hidr_scr, *, lb):
    fr = fr_ref[0]

    def hidden(z):
        h = jnp.sin(fr * (jnp.dot(z, w1_ref[0], preferred_element_type=F32, precision=HIGHEST) + b1_ref[0]))
        return jnp.sin(fr * (jnp.dot(h, w2_ref[0], preferred_element_type=F32, precision=HIGHEST) + b2_ref[0]))

    @pl.when(pl.program_id(1) == 0)
    def _():
        hid_scr[...] = hidden(z_ref[...])
        hidr_scr[...] = hidden(zr_ref[...])

    hn = jnp.dot(hid_scr[...], w3_ref[0], preferred_element_type=F32, precision=HIGHEST)
    hr = jnp.dot(hidr_scr[...], w3_ref[0], preferred_element_type=F32, precision=HIGHEST)
    dec, decr = dec_ref[...], decr_ref[...]
    hf, hb = hn[:, :D_HY] * dec, hn[:, D_HY:] * dec
    hfr, hbr = hr[:, :D_HY] * decr, hr[:, D_HY:] * decr
    taps = ((hf[:lb], hb[:lb]),
            (hf[lb:], hfr),
            (hbr, hb[lb:]))
    row = lax.broadcasted_iota(jnp.int32, (lb, 1), 0)
    cm, sm = cm_ref[...], sm_ref[...]
    ch, sh = ch_ref[...], sh_ref[...]
    scale = 1.0 / lb
    for d, (fwd, bwd) in enumerate(taps):
        bwd = jnp.where(row > 0, bwd, 0.0)
        gp = (fwd + bwd).astype(BF16)
        gm = (fwd - bwd).astype(BF16)
        o_ref[0, d, 0] = (_bdot(cm, gp) * ch + _bdot(sm, gp) * sh) * scale
        o_ref[0, d, 1] = (_bdot(cm, gm) * sh - _bdot(sm, gm) * ch) * scale


def _hyena_filters_blocked(seq, lb, w1p, b1p, frp, w2p, b2p, w3, cm, sm):
    assert seq == 2 * lb
    layers = w1p.shape[0]
    zp, decay = _filter_tables(seq)
    zr, decr = _filter_tables_reversed(seq, lb)
    _, _, ch, sh = _dft_tables(lb)
    per_layer = lambda l, o: (l, 0, 0)
    out = pl.pallas_call(
        functools.partial(_filter_blocked_kernel, lb=lb),
        out_shape=jax.ShapeDtypeStruct((layers * HY_ORDER, 3, 2, lb, D_HY), F32),
        grid=(layers, HY_ORDER),
        in_specs=[_resident((seq, LANES)), _resident((lb, LANES)),
                  pl.BlockSpec((1, LANES, LANES), per_layer), pl.BlockSpec((1, 1, LANES), per_layer),
                  pl.BlockSpec((1, 1, LANES), per_layer), pl.BlockSpec((1, LANES, LANES), per_layer),
                  pl.BlockSpec((1, 1, LANES), per_layer),
                  pl.BlockSpec((1, LANES, 2 * D_HY), lambda l, o: (l, 0, o)),
                  _resident((seq, D_HY)), _resident((lb, D_HY)), _resident((lb, lb)), _resident((lb, lb)),
                  _resident((lb, 1)), _resident((lb, 1))],
        out_specs=pl.BlockSpec((1, 3, 2, lb, D_HY), lambda l, o: (l * HY_ORDER + o, 0, 0, 0, 0)),
        scratch_shapes=[pltpu.VMEM((seq, LANES), F32), pltpu.VMEM((lb, LANES), F32)],
        compiler_params=_params(("arbitrary", "arbitrary")),
        name=f"hyena_block_filter_spectra_{seq}",
    )(jnp.asarray(zp), jnp.asarray(zr), w1p, b1p, frp, w2p, b2p, w3, jnp.asarray(decay), jnp.asarray(decr),
      cm, sm, jnp.asarray(ch), jnp.asarray(sh))
    return out.reshape(layers, HY_ORDER, 3, 2, lb, D_HY)


def _prenorm(x_ref, mod_ref, g_ref, shift_col, scale_col):
    d = D_MODEL
    shift = mod_ref[0, :, shift_col * d:(shift_col + 1) * d]
    scale = mod_ref[0, :, scale_col * d:(scale_col + 1) * d]
    return (_rmsnorm_rows(x_ref[...], g_ref[...]) * (1.0 + scale) + shift).astype(BF16)


def _qk_head_blocks(p, gain, rope):
    out = []
    for h in range(ATT_HEADS):
        xb = p[:, h * LANES:(h + 1) * LANES]
        xb = xb * lax.rsqrt(_group_sum(xb * xb) * (1.0 / QK_DIM) + EPS) * gain
        if rope is not None:
            cos, sin_lo, sin_hi = rope
            xb = (xb * cos + pltpu.roll(xb, LANES - ROPE_PAIRS, 1) * sin_lo
                  + pltpu.roll(xb, ROPE_PAIRS, 1) * sin_hi)
        out.append(xb)
    return jnp.concatenate(out, axis=-1)


def _in_kernel(*refs, tm, use_rope):
    if use_rope:
        (x_ref, mod_ref, g_ref, w_ref, cos_ref, slo_ref, shi_ref, qn_ref, kn_ref, lng_ref, lnb_ref,
         wcat_ref, bst_ref, pa_ref, osg_ref, q_ref, k_ref, v_ref) = refs
        rope = (cos_ref[...], slo_ref[...], shi_ref[...])
    else:
        (x_ref, mod_ref, g_ref, w_ref, qn_ref, kn_ref, lng_ref, lnb_ref,
         wcat_ref, bst_ref, pa_ref, osg_ref, q_ref, k_ref, v_ref) = refs
        rope = None
    hb = _prenorm(x_ref, mod_ref, g_ref, 0, 1)

    c0 = IN_A + IN_B
    pb = _bdot(hb, w_ref[:, IN_A:c0])
    pq = _bdot(hb, w_ref[:, c0:c0 + D_QK])
    pk = _bdot(hb, w_ref[:, c0 + D_QK:c0 + 2 * D_QK])

    pb = jax.nn.gelu(pb)
    u = pb[:, :D_SG]
    vn = []
    for j in range(D_SG // LANES):
        xb = pb[:, D_SG + j * LANES:D_SG + (j + 1) * LANES]
        xc = xb - _group_sum(xb) * (1.0 / QK_DIM)
        vn.append(xc * lax.rsqrt(_group_sum(xc * xc) * (1.0 / QK_DIM) + EPS))
    vn = (jnp.concatenate(vn, axis=-1) * lng_ref[...] + lnb_ref[...]).astype(BF16)
    lane = lax.broadcasted_iota(jnp.int32, (1, D_SG), 1)
    grp = lane // (D_SG // SG_GROUPS)
    zero = jnp.zeros((SG_CHUNK, D_SG), BF16)
    for ci in range(tm // SG_CHUNK):
        rows = slice(ci * SG_CHUNK, (ci + 1) * SG_CHUNK)
        vc = vn[rows]
        stacked = jnp.concatenate([jnp.where(grp == g, vc, zero) for g in range(SG_GROUPS)], axis=0)
        s = _bdot(wcat_ref[...], stacked) + bst_ref[...]
        osg_ref[rows, :] = (u[rows] * s).astype(BF16)

    q = _qk_head_blocks(pq, qn_ref[...], rope)
    q_ref[...] = (q * (QK_DIM ** -0.5 * LOG2E)).astype(BF16)
    pa_ref[...] = _bdot(hb, w_ref[:, :IN_A])
    k_ref[...] = _qk_head_blocks(pk, kn_ref[...], rope).astype(BF16)
    v_ref[...] = _v_aug(_bdot(hb, w_ref[:, c0 + 2 * D_QK:]))


def _in_kv_kernel(x_ref, mod_ref, g_ref, w_ref, kn_ref, k_ref, v_ref):
    hb = _prenorm(x_ref, mod_ref, g_ref, 0, 1)
    k = _qk_head_blocks(_bdot(hb, w_ref[:, :D_QK]), kn_ref[...], None)
    k_ref[...] = k.astype(BF16)
    v_ref[...] = _v_aug(_bdot(hb, w_ref[:, D_QK:]))


def _mod_spec(mod_row):
    return pl.BlockSpec((1, 1, 6 * D_MODEL), lambda i: (mod_row(i), 0, 0))


def _in_proj(xs, modl, g1, w_in_b, qn, kn, lng, lnb, wcat, bstab, *, seq, tm, use_rope, mod_row):
    n, d = xs.shape
    row = lambda width: pl.BlockSpec((tm, width), lambda i: (i, 0))
    vec = lambda width: pl.BlockSpec((1, width), lambda i: (0, 0))
    in_specs = [row(d), _mod_spec(mod_row), vec(d), _resident((d, D_IN))]
    args = [xs, modl, g1, w_in_b]
    if use_rope:
        nblk = seq // tm
        in_specs += [pl.BlockSpec((tm, LANES), lambda i: (i % nblk, 0))] * 3
        args += [jnp.asarray(t) for t in _rope_tables(seq)]
    in_specs += [vec(LANES), vec(LANES), vec(D_SG), vec(D_SG), _resident((SG_CHUNK, SG_GROUPS * SG_CHUNK)),
                 _resident((SG_CHUNK, D_SG))]
    args += [qn, kn, lng, lnb, wcat, bstab]
    return pl.pallas_call(
        functools.partial(_in_kernel, tm=tm, use_rope=use_rope),
        out_shape=(jax.ShapeDtypeStruct((n, IN_A), F32), jax.ShapeDtypeStruct((n, D_SG), BF16),
                   jax.ShapeDtypeStruct((n, D_QK), BF16), jax.ShapeDtypeStruct((n, D_QK), BF16),
                   jax.ShapeDtypeStruct((n, D_VAUG), BF16)),
        grid=(n // tm,),
        in_specs=in_specs,
        out_specs=(row(IN_A), row(D_SG), row(D_QK), row(D_QK), row(D_VAUG)),
        compiler_params=_params(("parallel",)),
        name="in_proj_rope" if use_rope else "in_proj",
    )(*args)


def _in_proj_kv(xs, modl, g1, w_kv_b, kn, *, tm, mod_row):
    n, d = xs.shape
    row = lambda width: pl.BlockSpec((tm, width), lambda i: (i, 0))
    vec = lambda width: pl.BlockSpec((1, width), lambda i: (0, 0))
    return pl.pallas_call(
        _in_kv_kernel,
        out_shape=(jax.ShapeDtypeStruct((n, D_QK), BF16), jax.ShapeDtypeStruct((n, D_VAUG), BF16)),
        grid=(n // tm,),
        in_specs=[row(d), _mod_spec(mod_row), vec(d), _resident((d, D_QK + D_ATT)), vec(LANES)],
        out_specs=(row(D_QK), row(D_VAUG)),
        compiler_params=_params(("parallel",)),
        name="in_proj_kv",
    )(xs, modl, g1, w_kv_b, kn)


def _att_kernel(*refs, segments, lam_init, tq):
    lam_ref, q_ref = refs[0], refs[1]
    kv_refs = refs[2:2 + 2 * segments]
    g_ref, o_ref = refs[2 + 2 * segments], refs[3 + 2 * segments]
    lv = lam_ref[...]
    lam = (jnp.exp(jnp.sum(lv[0:1] * lv[1:2], axis=-1, keepdims=True))
           - jnp.exp(jnp.sum(lv[2:3] * lv[3:4], axis=-1, keepdims=True)) + lam_init)
    lo = lax.broadcasted_iota(jnp.int32, (1, LANES), 1) < QK_DIM
    nt = (((1,), (1,)), ((), ()))

    def head_scores(h):
        cols = slice(h * LANES, (h + 1) * LANES)
        qh = q_ref[0, :, cols]
        zero = jnp.zeros_like(qh)
        return [[lax.dot_general(qm, kv_refs[2 * i][0, :, cols], nt, preferred_element_type=F32)
                 for i in range(segments)]
                for qm in (jnp.where(lo, qh, zero), jnp.where(lo, zero, qh))]

    nxt = head_scores(0)
    for h in range(ATT_HEADS):
        cols = slice(h * LANES, (h + 1) * LANES)
        scores, nxt = nxt, (head_scores(h + 1) if h + 1 < ATT_HEADS else None)
        o = []
        for s in scores:
            m = jnp.max(s[0], axis=-1, keepdims=True)
            for si in s[1:]:
                m = jnp.maximum(m, jnp.max(si, axis=-1, keepdims=True))
            acc = None
            for i in range(segments):
                p = jnp.exp2((s[i] - m).astype(BF16))
                a_i = _bdot(p, kv_refs[2 * i + 1][0, :, 2 * h * V_DIM:2 * (h + 1) * V_DIM])
                acc = a_i if acc is None else acc + a_i
            o.append(acc[:, :V_DIM] / acc[:, V_DIM:])
        out = o[0] - lam * o[1]
        o_ref[0, :, cols] = (_rmsnorm_rows(out, g_ref[...]) * (1.0 - lam_init)).astype(BF16)


def _attention(lamv, q, kvs, subg, *, tq, lam_init):
    b, t, _ = q.shape
    in_specs = [pl.BlockSpec((4, LANES), lambda i, j: (0, 0)),
                pl.BlockSpec((1, tq, D_QK), lambda i, j: (i, j, 0))]
    args = [lamv, q]
    for k, v, pieces in kvs:
        tk = k.shape[1] // pieces
        for piece in range(pieces):
            in_specs += [pl.BlockSpec((1, tk, D_QK), lambda i, j, piece=piece: (i, piece, 0)),
                         pl.BlockSpec((1, tk, D_VAUG), lambda i, j, piece=piece: (i, piece, 0))]
            args += [k, v]
    in_specs.append(pl.BlockSpec((1, V_DIM), lambda i, j: (0, 0)))
    args.append(subg)
    nseg = (len(in_specs) - 3) // 2
    return pl.pallas_call(
        functools.partial(_att_kernel, segments=nseg, lam_init=lam_init, tq=tq),
        out_shape=jax.ShapeDtypeStruct((b, t, D_ATT), BF16),
        grid=(b, t // tq),
        in_specs=in_specs,
        out_specs=pl.BlockSpec((1, tq, D_ATT), lambda i, j: (i, j, 0)),
        compiler_params=_params(("parallel", "parallel")),
        name=f"diff_attention_{nseg}seg",
    )(*args)


def _hyena_kernel(pa_ref, cw_ref, cb_ref, cm_ref, sm_ref, kt_ref, bias_ref, o_ref,
                  p_scr, z_scr, zb_scr, pb_scr, qb_scr, *, seq, rb):
    row = lax.broadcasted_iota(jnp.int32, (seq, 1), 0)
    for j in range(IN_A // D_HY):
        cols = slice(j * D_HY, (j + 1) * D_HY)
        pa = pa_ref[0, :, cols]
        prev = jnp.where(row > 0, pltpu.roll(pa, 1, 0), 0.0)
        nxt = jnp.where(row < seq - 1, pltpu.roll(pa, seq - 1, 0), 0.0)
        p_scr[:, cols] = (prev * cw_ref[0:1, cols] + pa * cw_ref[1:2, cols] + nxt * cw_ref[2:3, cols]
                          + cb_ref[:, cols])

    z_scr[...] = p_scr[:, :D_HY]
    for o in range(HY_ORDER):
        zb_scr[...] = z_scr[...].astype(BF16)
        gate_cols = slice((o + 1) * D_HY, (o + 2) * D_HY)

        def spectrum(i, carry):
            rows = pl.ds(pl.multiple_of(i * rb, rb), rb)
            a = _bdot(cm_ref[rows, :], zb_scr[...])
            b = _bdot(sm_ref[rows, :], zb_scr[...])
            kre, kim = kt_ref[2 * o, rows, :], kt_ref[2 * o + 1, rows, :]
            pb_scr[rows, :] = (a * kre + b * kim).astype(BF16)
            qb_scr[rows, :] = (b * kre - a * kim).astype(BF16)
            return carry

        lax.fori_loop(0, seq // rb, spectrum, 0)

        def synth(i, carry):
            rows = pl.ds(pl.multiple_of(i * rb, rb), rb)
            y = _bdot(cm_ref[rows, :], pb_scr[...]) + _bdot(sm_ref[rows, :], qb_scr[...])
            z_scr[rows, :] = p_scr[rows, gate_cols] * (y + bias_ref[o:o + 1, :] * z_scr[rows, :])
            return carry

        lax.fori_loop(0, seq // rb, synth, 0)
    o_ref[0] = z_scr[...].astype(BF16)


def _hyena(pa, cw, cb, cm, sm, ktab, bias, *, rb):
    b, seq, _ = pa.shape
    return pl.pallas_call(
        functools.partial(_hyena_kernel, seq=seq, rb=rb),
        out_shape=jax.ShapeDtypeStruct((b, seq, D_HY), BF16),
        grid=(b,),
        in_specs=[pl.BlockSpec((1, seq, IN_A), lambda i: (i, 0, 0), pipeline_mode=pl.Buffered(1)),
                  _resident((3, IN_A)), _resident((1, IN_A)), _resident((seq, seq)), _resident((seq, seq)),
                  _resident((2 * HY_ORDER, seq, D_HY)), _resident((HY_ORDER, D_HY))],
        out_specs=pl.BlockSpec((1, seq, D_HY), lambda i: (i, 0, 0)),
        scratch_shapes=[pltpu.VMEM((seq, IN_A), F32), pltpu.VMEM((seq, D_HY), F32),
                        pltpu.VMEM((seq, D_HY), BF16), pltpu.VMEM((seq, D_HY), BF16),
                        pltpu.VMEM((seq, D_HY), BF16)],
        compiler_params=_params(("parallel",)),
        name=f"hyena_mixer_{seq}",
    )(pa, cw, cb, cm, sm, ktab, bias)


def _hyena_blocked_kernel(pa_ref, cw_ref, cb_ref, cm_ref, sm_ref, kt_ref, bias_ref, o_ref,
                          p_scr, z_scr, zb_scr, pb_scr, qb_scr, *, seq, lb, rb):
    row = lax.broadcasted_iota(jnp.int32, (seq, 1), 0)
    for j in range(IN_A // D_HY):
        cols = slice(j * D_HY, (j + 1) * D_HY)
        pa = pa_ref[0, :, cols]
        prev = jnp.where(row > 0, pltpu.roll(pa, 1, 0), 0.0)
        nxt = jnp.where(row < seq - 1, pltpu.roll(pa, seq - 1, 0), 0.0)
        p_scr[:, cols] = (prev * cw_ref[0:1, cols] + pa * cw_ref[1:2, cols] + nxt * cw_ref[2:3, cols]
                          + cb_ref[:, cols])

    per_block = lb // rb
    z_scr[...] = p_scr[:, :D_HY]
    for o in range(HY_ORDER):
        zb_scr[...] = z_scr[...].astype(BF16)
        gate_cols = slice((o + 1) * D_HY, (o + 2) * D_HY)

        def spectrum(i, carry):
            rows = pl.ds(pl.multiple_of(i * rb, rb), rb)
            cmr, smr = cm_ref[rows, :], sm_ref[rows, :]
            a0, b0 = _bdot(cmr, zb_scr[0:lb, :]), _bdot(smr, zb_scr[0:lb, :])
            a1, b1 = _bdot(cmr, zb_scr[lb:seq, :]), _bdot(smr, zb_scr[lb:seq, :])
            k0r, k0i = kt_ref[o, 0, 0, rows, :], kt_ref[o, 0, 1, rows, :]
            kpr, kpi = kt_ref[o, 1, 0, rows, :], kt_ref[o, 1, 1, rows, :]
            kmr, kmi = kt_ref[o, 2, 0, rows, :], kt_ref[o, 2, 1, rows, :]
            pb_scr[0, rows, :] = (a0 * k0r + b0 * k0i + a1 * kmr + b1 * kmi).astype(BF16)
            qb_scr[0, rows, :] = (b0 * k0r - a0 * k0i + b1 * kmr - a1 * kmi).astype(BF16)
            pb_scr[1, rows, :] = (a1 * k0r + b1 * k0i + a0 * kpr + b0 * kpi).astype(BF16)
            qb_scr[1, rows, :] = (b1 * k0r - a1 * k0i + b0 * kpr - a0 * kpi).astype(BF16)
            return carry

        lax.fori_loop(0, per_block, spectrum, 0)

        def synth(i, carry):
            blk = i // per_block
            rows_in = pl.ds(pl.multiple_of((i % per_block) * rb, rb), rb)
            rows_out = pl.ds(pl.multiple_of(i * rb, rb), rb)
            y = _bdot(cm_ref[rows_in, :], pb_scr[blk]) + _bdot(sm_ref[rows_in, :], qb_scr[blk])
            z_scr[rows_out, :] = p_scr[rows_out, gate_cols] * (y + bias_ref[o:o + 1, :] * z_scr[rows_out, :])
            return carry

        lax.fori_loop(0, seq // rb, synth, 0)
    o_ref[0] = z_scr[...].astype(BF16)


def _hyena_blocked(pa, cw, cb, cm, sm, ktab, bias, *, lb, rb):
    b, seq, _ = pa.shape
    return pl.pallas_call(
        functools.partial(_hyena_blocked_kernel, seq=seq, lb=lb, rb=rb),
        out_shape=jax.ShapeDtypeStruct((b, seq, D_HY), BF16),
        grid=(b,),
        in_specs=[pl.BlockSpec((1, seq, IN_A), lambda i: (i, 0, 0), pipeline_mode=pl.Buffered(1)),
                  _resident((3, IN_A)), _resident((1, IN_A)), _resident((lb, lb)), _resident((lb, lb)),
                  _resident((HY_ORDER, 3, 2, lb, D_HY)), _resident((HY_ORDER, D_HY))],
        out_specs=pl.BlockSpec((1, seq, D_HY), lambda i: (i, 0, 0)),
        scratch_shapes=[pltpu.VMEM((seq, IN_A), F32), pltpu.VMEM((seq, D_HY), F32),
                        pltpu.VMEM((seq, D_HY), BF16), pltpu.VMEM((2, lb, D_HY), BF16),
                        pltpu.VMEM((2, lb, D_HY), BF16)],
        compiler_params=_params(("parallel",)),
        name=f"hyena_block_mixer_{seq}",
    )(pa, cw, cb, cm, sm, ktab, bias)


def _out_kernel(ohy_ref, osg_ref, oat_ref, x_ref, mod_ref, g2_ref, w_ref, nx_ref, h2_ref):
    d = D_MODEL
    mix = (_bdot(ohy_ref[...], w_ref[:D_HY]) + _bdot(osg_ref[...], w_ref[D_HY:D_HY + D_SG])
           + _bdot(oat_ref[...], w_ref[D_HY + D_SG:]))
    nx = x_ref[...] + mod_ref[0, :, 2 * d:3 * d] * mix
    nx_ref[...] = nx
    h2 = _rmsnorm_rows(nx, g2_ref[...]) * (1.0 + mod_ref[0, :, 4 * d:5 * d]) + mod_ref[0, :, 3 * d:4 * d]
    h2_ref[...] = h2.astype(BF16)


def _out_proj(ohy, osg, oat, xs, modl, g2, w_out_b, *, tm, mod_row):
    n, d = xs.shape
    row = lambda width: pl.BlockSpec((tm, width), lambda i: (i, 0))
    return pl.pallas_call(
        _out_kernel,
        out_shape=(jax.ShapeDtypeStruct((n, d), F32), jax.ShapeDtypeStruct((n, d), BF16)),
        grid=(n // tm,),
        in_specs=[row(D_HY), row(D_SG), row(D_ATT), row(d), _mod_spec(mod_row),
                  pl.BlockSpec((1, d), lambda i: (0, 0)), _resident((D_MIX, d))],
        out_specs=(row(d), row(d)),
        compiler_params=_params(("parallel",)),
        name="out_proj",
    )(ohy, osg, oat, xs, modl, g2, w_out_b)


def _ffn_kernel(h_ref, hp_ref, hn_ref, nx_ref, mod_ref, wu_ref, cw_ref, cb_ref, wd_ref, o_ref,
                hc_scr, ga_scr, va_scr, gb_scr, vb_scr, *, tm, seq):
    d = D_MODEL
    half = d // 2
    nchunk = D_FF // FF_CHUNK
    rows_ext = tm + 2 * HALO
    start = (pl.program_id(0) * tm) % seq
    hp, hn = hp_ref[...], hn_ref[...]
    hc_scr[0:HALO, :] = jnp.where(start == 0, jnp.zeros_like(hp), hp)
    hc_scr[HALO:HALO + tm, :] = h_ref[...]
    hc_scr[HALO + tm:rows_ext, :] = jnp.where(start + tm == seq, jnp.zeros_like(hn), hn)

    def chunk_cols(j):
        return pl.ds(pl.multiple_of(j * FF_CHUNK, FF_CHUNK), FF_CHUNK)

    def up(j, g_scr, v_scr):
        hc = hc_scr[...]
        g_scr[...] = _bdot(hc, wu_ref[:, chunk_cols(j)])
        v_scr[...] = _bdot(hc, wu_ref[:, chunk_cols(nchunk + j)])

    def conv(u_scr, j):
        u = u_scr[...]
        w = cw_ref[:, chunk_cols(j)]
        prev = pltpu.roll(u, 1, 0)[HALO:HALO + tm]
        nxt = pltpu.roll(u, rows_ext - 1, 0)[HALO:HALO + tm]
        return prev * w[0:1] + u[HALO:HALO + tm] * w[1:2] + nxt * w[2:3] + cb_ref[:, chunk_cols(j)]

    def activate(j, g_scr, v_scr):
        gate = conv(g_scr, j)
        val = conv(v_scr, nchunk + j)
        return (gate * jax.nn.sigmoid(gate) * val).astype(BF16)

    def down(j, act):
        o_ref[:, :half] += _bdot(act, wd_ref[j, :, :half])
        o_ref[:, half:] += _bdot(act, wd_ref[j, :, half:])

    o_ref[...] = jnp.zeros_like(o_ref)
    up(0, ga_scr, va_scr)

    def pair(i, carry):
        j = 2 * i
        act = activate(j, ga_scr, va_scr)
        up(j + 1, gb_scr, vb_scr)
        down(j, act)
        act = activate(j + 1, gb_scr, vb_scr)
        up(j + 2, ga_scr, va_scr)
        down(j + 1, act)
        return carry

    lax.fori_loop(0, (nchunk - 1) // 2, pair, 0)
    down(nchunk - 1, activate(nchunk - 1, ga_scr, va_scr))
    o_ref[...] = nx_ref[...] + mod_ref[0, :, 5 * d:6 * d] * o_ref[...]


def _ffn(h2, nx, modl, wu, cw, cb, wd, *, tm, seq, mod_row):
    n, d = nx.shape
    assert seq % tm == 0 and (D_FF // FF_CHUNK) % 2 == 1
    nh = n // HALO
    per = tm // HALO
    rows_ext = tm + 2 * HALO
    row = lambda width: pl.BlockSpec((tm, width), lambda i: (i, 0))
    return pl.pallas_call(
        functools.partial(_ffn_kernel, tm=tm, seq=seq),
        scratch_shapes=[pltpu.VMEM((rows_ext, d), BF16)] + [pltpu.VMEM((rows_ext, FF_CHUNK), F32)] * 4,
        out_shape=jax.ShapeDtypeStruct((n, d), F32),
        grid=(n // tm,),
        in_specs=[row(d),
                  pl.BlockSpec((HALO, d), lambda i: (jnp.maximum(i * per - 1, 0), 0)),
                  pl.BlockSpec((HALO, d), lambda i: (jnp.minimum((i + 1) * per, nh - 1), 0)),
                  row(d), _mod_spec(mod_row),
                  _resident(wu.shape), _resident(cw.shape), _resident(cb.shape), _resident(wd.shape)],
        out_specs=row(d),
        compiler_params=_params(("parallel",)),
        name="conv_ffn",
    )(h2, h2, h2, nx, modl, wu, cw, cb, wd)


def _pad_to(a, shape):
    return jnp.pad(a, [(0, t - s) for s, t in zip(a.shape, shape)])


def kernel(x, c, ctx, c_ctx, w_mod, b_mod, norm1_g, w_in, hy_conv_w, hy_conv_b, hy_w1, hy_b1, hy_w2, hy_b2,
           hy_w3, hy_freq, hy_bias, sg_ln_g, sg_ln_b, sg_w, sg_b, q_norm_g, k_norm_g, lam_q1, lam_k1, lam_q2,
           lam_k2, subln_g, w_out, norm2_g, ffn_w_up, ffn_conv_w, ffn_conv_b, ffn_w_down):
    bsz, seq, d = x.shape
    lc = ctx.shape[1]
    depth = w_mod.shape[0]
    assert d == D_MODEL and bsz + 1 <= MOD_ROWS and (bsz * lc) % 512 == 0 and seq % 512 == 0

    cc = _pad_to(jnp.concatenate([c, c_ctx[None, :]], axis=0), (MOD_ROWS, d))
    mod = _modulation(cc, w_mod, b_mod)
    ctx_row = bsz

    w1p = _pad_to(hy_w1, (depth, LANES, LANES))
    b1p = _pad_to(hy_b1, (depth, LANES)).reshape(depth, 1, LANES)
    frp = _pad_to(hy_freq, (depth, LANES)).reshape(depth, 1, LANES)
    w2p = _pad_to(hy_w2, (depth, LANES, LANES))
    b2p = _pad_to(hy_b2, (depth, LANES)).reshape(depth, 1, LANES)
    w3p = _pad_to(hy_w3, (depth, LANES, HY_ORDER * 2 * D_HY))
    lb = seq // 2
    dft = {}
    for s in (lb, lc):
        cmat, smat, _, _ = _dft_tables(s)
        dft[s] = (jnp.asarray(cmat).astype(BF16), jnp.asarray(smat).astype(BF16))
    ktab = _hyena_filters_blocked(seq, lb, w1p, b1p, frp, w2p, b2p, w3p, *dft[lb])
    nctx = depth - 1
    ktab_c = _hyena_filters(lc, w1p[:nctx], b1p[:nctx], frp[:nctx], w2p[:nctx], b2p[:nctx], w3p[:nctx], *dft[lc])

    tm = 512
    nchunk = D_FF // FF_CHUNK
    xs = x.reshape(bsz * seq, d)
    cs = ctx.reshape(bsz * lc, d)
    lat_row = lambda i: (i * tm) // seq
    ctx_mod = lambda i: ctx_row

    for l in range(depth):
        last = l == depth - 1
        lam_init = 0.8 - 0.6 * math.exp(-0.3 * l)
        modl = mod[l].reshape(MOD_ROWS, 1, 6 * d)
        g1 = norm1_g[l].reshape(1, d)
        g2 = norm2_g[l].reshape(1, d)
        w_in_b = w_in[l].astype(BF16)
        w_out_b = w_out[l].astype(BF16)
        wu = ffn_w_up[l].astype(BF16)
        wd = ffn_w_down[l].astype(BF16).reshape(nchunk, FF_CHUNK, d)
        fcw = ffn_conv_w[l]
        fcb = ffn_conv_b[l].reshape(1, 2 * D_FF)
        qn = jnp.tile(q_norm_g[l], LANES // QK_DIM).reshape(1, LANES)
        kn = jnp.tile(k_norm_g[l], LANES // QK_DIM).reshape(1, LANES)
        lng = sg_ln_g[l].reshape(1, D_SG)
        lnb = sg_ln_b[l].reshape(1, D_SG)
        wcat = sg_w[l].transpose(1, 0, 2).reshape(SG_CHUNK, SG_GROUPS * SG_CHUNK).astype(BF16)
        bstab = jnp.repeat(sg_b[l].T, D_SG // SG_GROUPS, axis=1)
        lamv = _pad_to(jnp.stack([lam_q1[l], lam_k1[l], lam_q2[l], lam_k2[l]]), (4, LANES))
        subg = subln_g[l].reshape(1, V_DIM)
        hcw = hy_conv_w[l]
        hcb = hy_conv_b[l].reshape(1, IN_A)
        sgu = (qn, kn, lng, lnb, wcat, bstab)

        pa, osg, q, k, v = _in_proj(xs, modl, g1, w_in_b, *sgu, seq=seq, tm=tm, use_rope=True, mod_row=lat_row)
        if last:
            kc, vc = _in_proj_kv(cs, modl, g1, w_in_b[:, IN_A + IN_B + D_QK:], kn, tm=tm, mod_row=ctx_mod)
        else:
            pa_c, osg_c, qc, kc, vc = _in_proj(cs, modl, g1, w_in_b, *sgu, seq=lc, tm=tm, use_rope=False,
                                               mod_row=ctx_mod)
        b3 = lambda a, t: a.reshape(bsz, t, a.shape[-1])
        kc3, vc3 = b3(kc, lc), b3(vc, lc)
        oat = _attention(lamv, b3(q, seq), [(b3(k, seq), b3(v, seq), ATT_KEY_PIECES), (kc3, vc3, 1)], subg,
                         tq=256, lam_init=lam_init)
        ohy = _hyena_blocked(b3(pa, seq), hcw, hcb, *dft[lb], ktab[l], hy_bias[l], lb=lb, rb=512)
        nx, h2 = _out_proj(ohy.reshape(-1, D_HY), osg, oat.reshape(-1, D_ATT), xs, modl, g2, w_out_b,
                           tm=tm, mod_row=lat_row)
        xs_next = _ffn(h2, nx, modl, wu, fcw, fcb, wd, tm=tm, seq=seq, mod_row=lat_row)

        if not last:
            oat_c = _attention(lamv, b3(qc, lc), [(kc3, vc3, 1)], subg, tq=lc, lam_init=lam_init)
            ohy_c = _hyena(b3(pa_c, lc), hcw, hcb, *dft[lc], ktab_c[l], hy_bias[l], rb=lc)
            nx_c, h2_c = _out_proj(ohy_c.reshape(-1, D_HY), osg_c, oat_c.reshape(-1, D_ATT), cs, modl, g2,
                                   w_out_b, tm=tm, mod_row=ctx_mod)
            cs = _ffn(h2_c, nx_c, modl, wu, fcw, fcb, wd, tm=min(tm, lc), seq=lc, mod_row=ctx_mod)
        xs = xs_next
    return xs.reshape(bsz, seq, d)
```

```python
import functools
import math

import jax
import jax.numpy as jnp
import numpy as np
from jax import lax
from jax.experimental import pallas as pl
from jax.experimental.pallas import tpu as pltpu

F32 = jnp.float32
BF16 = jnp.bfloat16

D_MODEL = 1024
DEPTH = 2
GRID_W = 64
EPS = 1e-6
D_HY = 256
HY_ORDER = 2
HY_BANDS = 16
HY_EMB = 1 + 2 * HY_BANDS
HY_FFN = 64
HY_DECAY_TARGET = 1e-2
HY_FAST_DECAY = 0.3
HY_SLOW_DECAY = 1.5
SG_GROUPS = 4
D_SG = 256
SG_CHUNK = 128
ATT_HEADS = 4
QK_DIM = 64
V_DIM = 128
D_ATT = ATT_HEADS * V_DIM
ROPE_PAIRS = QK_DIM // 4
ROPE_BASE = 10000.0
D_MIX = D_HY + D_SG + D_ATT
IN_A = 3 * D_HY
IN_B = 2 * D_SG
D_QK = 2 * ATT_HEADS * QK_DIM
IN_C = 2 * D_QK + D_ATT
D_IN = IN_A + IN_B + IN_C
D_FF = 2816
D_VAUG = 2 * D_ATT
LOG2E = 1.4426950408889634

LANES = 128
BF16_SUBLANES = 16
VMEM_LIMIT_BYTES = 56 * 1024 * 1024

MOD_ROWS = 16
FF_CHUNK = 256
ATT_KEY_PIECES = 1
HALO = BF16_SUBLANES


def _params(sem, vmem=VMEM_LIMIT_BYTES):
    return pltpu.CompilerParams(dimension_semantics=sem, vmem_limit_bytes=vmem)


def _resident(shape):
    nd = len(shape)
    return pl.BlockSpec(shape, lambda *_: (0,) * nd, pipeline_mode=pl.Buffered(1))


@functools.lru_cache(maxsize=None)
def _dft_tables(seq):
    n2 = 2 * seq
    f = np.arange(seq, dtype=np.int64)
    m = ((2 * f[:, None] + 1) * (2 * f[None, :] + 1)) % (4 * n2)
    ang = (2.0 * np.pi / (4 * n2)) * m.astype(np.float64)
    half = np.pi * (2 * f + 1) / (2.0 * n2)
    return (np.cos(ang).astype(np.float32), np.sin(ang).astype(np.float32),
            np.cos(half).astype(np.float32)[:, None], np.sin(half).astype(np.float32)[:, None])


@functools.lru_cache(maxsize=None)
def _filter_tables(seq):
    t = np.linspace(0.0, 1.0, seq, dtype=np.float32)[:, None]
    t_r = np.arange(seq, dtype=np.float32)[:, None]
    bands = np.linspace(1e-4, HY_BANDS - 1, HY_BANDS, dtype=np.float32)[None, :]
    w = (2.0 * math.pi * t_r / seq).astype(np.float32)
    z = np.concatenate([t, np.cos(bands * w), -np.sin(bands * w)], axis=-1).astype(np.float32)
    zp = np.zeros((seq, LANES), np.float32)
    zp[:, :HY_EMB] = z
    min_decay = math.log(HY_DECAY_TARGET) / HY_SLOW_DECAY
    max_decay = math.log(HY_DECAY_TARGET) / HY_FAST_DECAY
    deltas = np.abs(np.linspace(min_decay, max_decay, D_HY, dtype=np.float32))
    decay = np.exp(-t * deltas[None, :]).astype(np.float32)
    return zp, decay


@functools.lru_cache(maxsize=None)
def _filter_tables_reversed(seq, lb):
    zp, decay = _filter_tables(seq)
    idx = lb - np.arange(lb)
    return np.ascontiguousarray(zp[idx]), np.ascontiguousarray(decay[idx])


@functools.lru_cache(maxsize=None)
def _rope_tables(seq):
    pos = np.arange(seq)
    row = (pos // GRID_W).astype(np.float32)
    col = (pos % GRID_W).astype(np.float32)
    inv = (ROPE_BASE ** (-np.arange(ROPE_PAIRS, dtype=np.float32) / ROPE_PAIRS)).astype(np.float32)
    lane = np.arange(LANES)
    axis = (lane % QK_DIM) // (2 * ROPE_PAIRS)
    half = (lane % (2 * ROPE_PAIRS)) // ROPE_PAIRS
    pair = lane % ROPE_PAIRS
    p = np.where(axis[None, :] == 0, row[:, None], col[:, None]).astype(np.float32)
    ang = (p * inv[pair][None, :]).astype(np.float32)
    cos, sin = np.cos(ang).astype(np.float32), np.sin(ang).astype(np.float32)
    sin_lo = np.where(half[None, :] == 0, -sin, 0.0).astype(np.float32)
    sin_hi = np.where(half[None, :] == 1, sin, 0.0).astype(np.float32)
    return cos, sin_lo, sin_hi


def _group_sum(x):
    lane = lax.broadcasted_iota(jnp.int32, (1, LANES), 1)
    lo = lane < QK_DIM
    s_lo = jnp.sum(jnp.where(lo, x, 0.0), axis=-1, keepdims=True)
    s_hi = jnp.sum(jnp.where(lo, 0.0, x), axis=-1, keepdims=True)
    return jnp.where(lo, s_lo, s_hi)


def _rmsnorm_rows(x, g):
    return x * lax.rsqrt(jnp.mean(x * x, axis=-1, keepdims=True) + EPS) * g


def _bdot(a, b):
    return jnp.dot(a, b, preferred_element_type=F32)


def _split_bf16(x):
    hi = x.astype(BF16)
    return hi, (x - hi.astype(F32)).astype(BF16)


def _dot3(a, b):
    a_hi, a_lo = _split_bf16(a)
    b_hi, b_lo = _split_bf16(b)
    return _bdot(a_hi, b_hi) + (_bdot(a_lo, b_hi) + _bdot(a_hi, b_lo))


def _v_aug(v):
    ones = jnp.ones((v.shape[0], V_DIM), BF16)
    vb = v.astype(BF16)
    parts = []
    for h in range(ATT_HEADS):
        parts += [vb[:, h * V_DIM:(h + 1) * V_DIM], ones]
    return jnp.concatenate(parts, axis=-1)


def _mod_kernel(cc_ref, w_ref, b_ref, o_ref):
    cc = cc_ref[...]
    a = cc * jax.nn.sigmoid(cc)
    o_ref[0] = _dot3(a, w_ref[0]) + b_ref[0]


def _modulation(cc, w_mod, b_mod):
    depth, d, n = w_mod.shape
    tn = 768
    return pl.pallas_call(
        _mod_kernel,
        out_shape=jax.ShapeDtypeStruct((depth, MOD_ROWS, n), F32),
        grid=(depth, n // tn),
        in_specs=[pl.BlockSpec((MOD_ROWS, d), lambda l, j: (0, 0)),
                  pl.BlockSpec((1, d, tn), lambda l, j: (l, 0, j)),
                  pl.BlockSpec((1, 1, tn), lambda l, j: (l, 0, j))],
        out_specs=pl.BlockSpec((1, MOD_ROWS, tn), lambda l, j: (l, 0, j)),
        compiler_params=_params(("parallel", "parallel")),
        name="adaln_modulation",
    )(cc, w_mod, b_mod.reshape(depth, 1, n))


def _filter_kernel(z_ref, w1_ref, b1_ref, fr_ref, w2_ref, b2_ref, w3_ref, dec_ref, cm_ref, sm_ref,
                   ch_ref, sh_ref, o_ref, *, seq):
    fr = fr_ref[0]
    h = jnp.sin(fr * (_dot3(z_ref[...], w1_ref[0]) + b1_ref[0]))
    h = jnp.sin(fr * (_dot3(h, w2_ref[0]) + b2_ref[0]))
    h = _dot3(h, w3_ref[0])
    dec = dec_ref[...]
    hf = h[:, :D_HY] * dec
    row = lax.broadcasted_iota(jnp.int32, (seq, 1), 0)
    hb = jnp.where(row > 0, h[:, D_HY:] * dec, 0.0)
    gp = (hf + hb).astype(BF16)
    gm = (hf - hb).astype(BF16)
    cm, sm = cm_ref[...], sm_ref[...]
    ch, sh = ch_ref[...], sh_ref[...]
    scale = 1.0 / seq
    o_ref[0, 0] = (_bdot(cm, gp) * ch + _bdot(sm, gp) * sh) * scale
    o_ref[0, 1] = (_bdot(cm, gm) * sh - _bdot(sm, gm) * ch) * scale


def _hyena_filters(seq, w1p, b1p, frp, w2p, b2p, w3, cm, sm):
    layers = w1p.shape[0]
    zp, decay = _filter_tables(seq)
    _, _, ch, sh = _dft_tables(seq)
    out = pl.pallas_call(
        functools.partial(_filter_kernel, seq=seq),
        out_shape=jax.ShapeDtypeStruct((layers * HY_ORDER, 2, seq, D_HY), F32),
        grid=(layers, HY_ORDER),
        in_specs=[_resident((seq, LANES)),
                  pl.BlockSpec((1, LANES, LANES), lambda l, o: (l, 0, 0)),
                  pl.BlockSpec((1, 1, LANES), lambda l, o: (l, 0, 0)),
                  pl.BlockSpec((1, 1, LANES), lambda l, o: (l, 0, 0)),
                  pl.BlockSpec((1, LANES, LANES), lambda l, o: (l, 0, 0)),
                  pl.BlockSpec((1, 1, LANES), lambda l, o: (l, 0, 0)),
                  pl.BlockSpec((1, LANES, 2 * D_HY), lambda l, o: (l, 0, o)),
                  _resident((seq, D_HY)), _resident((seq, seq)), _resident((seq, seq)),
                  _resident((seq, 1)), _resident((seq, 1))],
        out_specs=pl.BlockSpec((1, 2, seq, D_HY), lambda l, o: (l * HY_ORDER + o, 0, 0, 0)),
        compiler_params=_params(("parallel", "parallel")),
        name=f"hyena_filter_spectra_{seq}",
    )(jnp.asarray(zp), w1p, b1p, frp, w2p, b2p, w3, jnp.asarray(decay), cm, sm, jnp.asarray(ch), jnp.asarray(sh))
    return out.reshape(layers, 2 * HY_ORDER, seq, D_HY)


def _filter_blocked_kernel(z_ref, zr_ref, w1_ref, b1_ref, fr_ref, w2_ref, b2_ref, w3_ref, dec_ref, decr_ref,
                           cm_ref, sm_ref, ch_ref, sh_ref, o_ref, hid_scr, hidr_scr, *, lb):
    fr = fr_ref[0]

    def hidden(z):
        h = jnp.sin(fr * (_dot3(z, w1_ref[0]) + b1_ref[0]))
        return jnp.sin(fr * (_dot3(h, w2_ref[0]) + b2_ref[0]))

    @pl.when(pl.program_id(1) == 0)
    def _():
        hid_scr[...] = hidden(z_ref[...])
        hidr_scr[...] = hidden(zr_ref[...])

    hall = _dot3(jnp.concatenate([hid_scr[...], hidr_scr[...]], axis=0), w3_ref[0])
    hn, hr = hall[:2 * lb], hall[2 * lb:]
    dec, decr = dec_ref[...], decr_ref[...]
    hf, hb = hn[:, :D_HY] * dec, hn[:, D_HY:] * dec
    hfr, hbr = hr[:, :D_HY] * decr, hr[:, D_HY:] * decr
    taps = ((hf[:lb], hb[:lb]),
            (hf[lb:], hfr),
            (hbr, hb[lb:]))
    row = lax.broadcasted_iota(jnp.int32, (lb, 1), 0)
    cm, sm = cm_ref[...], sm_ref[...]
    ch, sh = ch_ref[...], sh_ref[...]
    scale = 1.0 / lb
    for d, (fwd, bwd) in enumerate(taps):
        bwd = jnp.where(row > 0, bwd, 0.0)
        gp = (fwd + bwd).astype(BF16)
        gm = (fwd - bwd).astype(BF16)
        o_ref[0, d, 0] = (_bdot(cm, gp) * ch + _bdot(sm, gp) * sh) * scale
        o_ref[0, d, 1] = (_bdot(cm, gm) * sh - _bdot(sm, gm) * ch) * scale


def _hyena_filters_blocked(seq, lb, w1p, b1p, frp, w2p, b2p, w3, cm, sm):
    assert seq == 2 * lb
    layers = w1p.shape[0]
    zp, decay = _filter_tables(seq)
    zr, decr = _filter_tables_reversed(seq, lb)
    _, _, ch, sh = _dft_tables(lb)
    per_layer = lambda l, o: (l, 0, 0)
    out = pl.pallas_call(
        functools.partial(_filter_blocked_kernel, lb=lb),
        out_shape=jax.ShapeDtypeStruct((layers * HY_ORDER, 3, 2, lb, D_HY), F32),
        grid=(layers, HY_ORDER),
        in_specs=[_resident((seq, LANES)), _resident((lb, LANES)),
                  pl.BlockSpec((1, LANES, LANES), per_layer), pl.BlockSpec((1, 1, LANES), per_layer),
                  pl.BlockSpec((1, 1, LANES), per_layer), pl.BlockSpec((1, LANES, LANES), per_layer),
                  pl.BlockSpec((1, 1, LANES), per_layer),
                  pl.BlockSpec((1, LANES, 2 * D_HY), lambda l, o: (l, 0, o)),
                  _resident((seq, D_HY)), _resident((lb, D_HY)), _resident((lb, lb)), _resident((lb, lb)),
                  _resident((lb, 1)), _resident((lb, 1))],
        out_specs=pl.BlockSpec((1, 3, 2, lb, D_HY), lambda l, o: (l * HY_ORDER + o, 0, 0, 0, 0)),
        scratch_shapes=[pltpu.VMEM((seq, LANES), F32), pltpu.VMEM((lb, LANES), F32)],
        compiler_params=_params(("arbitrary", "arbitrary")),
        name=f"hyena_block_filter_spectra_{seq}",
    )(jnp.asarray(zp), jnp.asarray(zr), w1p, b1p, frp, w2p, b2p, w3, jnp.asarray(decay), jnp.asarray(decr),
      cm, sm, jnp.asarray(ch), jnp.asarray(sh))
    return out.reshape(layers, HY_ORDER, 3, 2, lb, D_HY)


def _prenorm(x_ref, mod_ref, g_ref, shift_col, scale_col):
    d = D_MODEL
    shift = mod_ref[0, :, shift_col * d:(shift_col + 1) * d]
    scale = mod_ref[0, :, scale_col * d:(scale_col + 1) * d]
    return (_rmsnorm_rows(x_ref[...], g_ref[...]) * (1.0 + scale) + shift).astype(BF16)


def _qk_head_blocks(p, gain, rope):
    out = []
    for h in range(ATT_HEADS):
        xb = p[:, h * LANES:(h + 1) * LANES]
        xb = xb * lax.rsqrt(_group_sum(xb * xb) * (1.0 / QK_DIM) + EPS) * gain
        if rope is not None:
            cos, sin_lo, sin_hi = rope
            xb = (xb * cos + pltpu.roll(xb, LANES - ROPE_PAIRS, 1) * sin_lo
                  + pltpu.roll(xb, ROPE_PAIRS, 1) * sin_hi)
        out.append(xb)
    return jnp.concatenate(out, axis=-1)


def _in_kernel(*refs, tm, use_rope):
    if use_rope:
        (x_ref, mod_ref, g_ref, w_ref, cos_ref, slo_ref, shi_ref, qn_ref, kn_ref, lng_ref, lnb_ref,
         wcat_ref, bst_ref, pa_ref, osg_ref, q_ref, k_ref, v_ref) = refs
        rope = (cos_ref[...], slo_ref[...], shi_ref[...])
    else:
        (x_ref, mod_ref, g_ref, w_ref, qn_ref, kn_ref, lng_ref, lnb_ref,
         wcat_ref, bst_ref, pa_ref, osg_ref, q_ref, k_ref, v_ref) = refs
        rope = None
    hb = _prenorm(x_ref, mod_ref, g_ref, 0, 1)

    c0 = IN_A + IN_B
    pb = _bdot(hb, w_ref[:, IN_A:c0])
    pq = _bdot(hb, w_ref[:, c0:c0 + D_QK])
    pk = _bdot(hb, w_ref[:, c0 + D_QK:c0 + 2 * D_QK])

    pb = jax.nn.gelu(pb)
    u = pb[:, :D_SG]
    vn = []
    for j in range(D_SG // LANES):
        xb = pb[:, D_SG + j * LANES:D_SG + (j + 1) * LANES]
        xc = xb - _group_sum(xb) * (1.0 / QK_DIM)
        vn.append(xc * lax.rsqrt(_group_sum(xc * xc) * (1.0 / QK_DIM) + EPS))
    vn = (jnp.concatenate(vn, axis=-1) * lng_ref[...] + lnb_ref[...]).astype(BF16)
    lane = lax.broadcasted_iota(jnp.int32, (1, D_SG), 1)
    grp = lane // (D_SG // SG_GROUPS)
    zero = jnp.zeros((SG_CHUNK, D_SG), BF16)
    for ci in range(tm // SG_CHUNK):
        rows = slice(ci * SG_CHUNK, (ci + 1) * SG_CHUNK)
        vc = vn[rows]
        stacked = jnp.concatenate([jnp.where(grp == g, vc, zero) for g in range(SG_GROUPS)], axis=0)
        s = _bdot(wcat_ref[...], stacked) + bst_ref[...]
        osg_ref[rows, :] = (u[rows] * s).astype(BF16)

    q = _qk_head_blocks(pq, qn_ref[...], rope)
    q_ref[...] = (q * (QK_DIM ** -0.5 * LOG2E)).astype(BF16)
    pa_ref[...] = _bdot(hb, w_ref[:, :IN_A])
    k_ref[...] = _qk_head_blocks(pk, kn_ref[...], rope).astype(BF16)
    v_ref[...] = _v_aug(_bdot(hb, w_ref[:, c0 + 2 * D_QK:]))


def _in_kv_kernel(x_ref, mod_ref, g_ref, w_ref, kn_ref, k_ref, v_ref):
    hb = _prenorm(x_ref, mod_ref, g_ref, 0, 1)
    k = _qk_head_blocks(_bdot(hb, w_ref[:, :D_QK]), kn_ref[...], None)
    k_ref[...] = k.astype(BF16)
    v_ref[...] = _v_aug(_bdot(hb, w_ref[:, D_QK:]))


def _mod_spec(mod_row):
    return pl.BlockSpec((1, 1, 6 * D_MODEL), lambda i: (mod_row(i), 0, 0))


def _in_proj(xs, modl, g1, w_in_b, qn, kn, lng, lnb, wcat, bstab, *, seq, tm, use_rope, mod_row):
    n, d = xs.shape
    row = lambda width: pl.BlockSpec((tm, width), lambda i: (i, 0))
    vec = lambda width: pl.BlockSpec((1, width), lambda i: (0, 0))
    in_specs = [row(d), _mod_spec(mod_row), vec(d), _resident((d, D_IN))]
    args = [xs, modl, g1, w_in_b]
    if use_rope:
        nblk = seq // tm
        in_specs += [pl.BlockSpec((tm, LANES), lambda i: (i % nblk, 0))] * 3
        args += [jnp.asarray(t) for t in _rope_tables(seq)]
    in_specs += [vec(LANES), vec(LANES), vec(D_SG), vec(D_SG), _resident((SG_CHUNK, SG_GROUPS * SG_CHUNK)),
                 _resident((SG_CHUNK, D_SG))]
    args += [qn, kn, lng, lnb, wcat, bstab]
    return pl.pallas_call(
        functools.partial(_in_kernel, tm=tm, use_rope=use_rope),
        out_shape=(jax.ShapeDtypeStruct((n, IN_A), F32), jax.ShapeDtypeStruct((n, D_SG), BF16),
                   jax.ShapeDtypeStruct((n, D_QK), BF16), jax.ShapeDtypeStruct((n, D_QK), BF16),
                   jax.ShapeDtypeStruct((n, D_VAUG), BF16)),
        grid=(n // tm,),
        in_specs=in_specs,
        out_specs=(row(IN_A), row(D_SG), row(D_QK), row(D_QK), row(D_VAUG)),
        compiler_params=_params(("parallel",)),
        name="in_proj_rope" if use_rope else "in_proj",
    )(*args)


def _in_proj_kv(xs, modl, g1, w_kv_b, kn, *, tm, mod_row):
    n, d = xs.shape
    row = lambda width: pl.BlockSpec((tm, width), lambda i: (i, 0))
    vec = lambda width: pl.BlockSpec((1, width), lambda i: (0, 0))
    return pl.pallas_call(
        _in_kv_kernel,
        out_shape=(jax.ShapeDtypeStruct((n, D_QK), BF16), jax.ShapeDtypeStruct((n, D_VAUG), BF16)),
        grid=(n // tm,),
        in_specs=[row(d), _mod_spec(mod_row), vec(d), _resident((d, D_QK + D_ATT)), vec(LANES)],
        out_specs=(row(D_QK), row(D_VAUG)),
        compiler_params=_params(("parallel",)),
        name="in_proj_kv",
    )(xs, modl, g1, w_kv_b, kn)


def _att_kernel(*refs, segments, lam_init, tq):
    lam_ref, q_ref = refs[0], refs[1]
    kv_refs = refs[2:2 + 2 * segments]
    g_ref, o_ref = refs[2 + 2 * segments], refs[3 + 2 * segments]
    lv = lam_ref[...]
    lam = (jnp.exp(jnp.sum(lv[0:1] * lv[1:2], axis=-1, keepdims=True))
           - jnp.exp(jnp.sum(lv[2:3] * lv[3:4], axis=-1, keepdims=True)) + lam_init)
    lo = lax.broadcasted_iota(jnp.int32, (1, LANES), 1) < QK_DIM
    nt = (((1,), (1,)), ((), ()))

    def head_scores(h):
        cols = slice(h * LANES, (h + 1) * LANES)
        qh = q_ref[0, :, cols]
        zero = jnp.zeros_like(qh)
        return [[lax.dot_general(qm, kv_refs[2 * i][0, :, cols], nt, preferred_element_type=F32)
                 for i in range(segments)]
                for qm in (jnp.where(lo, qh, zero), jnp.where(lo, zero, qh))]

    nxt = head_scores(0)
    for h in range(ATT_HEADS):
        cols = slice(h * LANES, (h + 1) * LANES)
        scores, nxt = nxt, (head_scores(h + 1) if h + 1 < ATT_HEADS else None)
        o = []
        for s in scores:
            m = jnp.max(s[0], axis=-1, keepdims=True)
            for si in s[1:]:
                m = jnp.maximum(m, jnp.max(si, axis=-1, keepdims=True))
            acc = None
            for i in range(segments):
                p = jnp.exp2((s[i] - m).astype(BF16))
                a_i = _bdot(p, kv_refs[2 * i + 1][0, :, 2 * h * V_DIM:2 * (h + 1) * V_DIM])
                acc = a_i if acc is None else acc + a_i
            o.append(acc[:, :V_DIM] / acc[:, V_DIM:])
        out = o[0] - lam * o[1]
        o_ref[0, :, cols] = (_rmsnorm_rows(out, g_ref[...]) * (1.0 - lam_init)).astype(BF16)


def _attention(lamv, q, kvs, subg, *, tq, lam_init):
    b, t, _ = q.shape
    in_specs = [pl.BlockSpec((4, LANES), lambda i, j: (0, 0)),
                pl.BlockSpec((1, tq, D_QK), lambda i, j: (i, j, 0))]
    args = [lamv, q]
    for k, v, pieces in kvs:
        tk = k.shape[1] // pieces
        for piece in range(pieces):
            in_specs += [pl.BlockSpec((1, tk, D_QK), lambda i, j, piece=piece: (i, piece, 0)),
                         pl.BlockSpec((1, tk, D_VAUG), lambda i, j, piece=piece: (i, piece, 0))]
            args += [k, v]
    in_specs.append(pl.BlockSpec((1, V_DIM), lambda i, j: (0, 0)))
    args.append(subg)
    nseg = (len(in_specs) - 3) // 2
    return pl.pallas_call(
        functools.partial(_att_kernel, segments=nseg, lam_init=lam_init, tq=tq),
        out_shape=jax.ShapeDtypeStruct((b, t, D_ATT), BF16),
        grid=(b, t // tq),
        in_specs=in_specs,
        out_specs=pl.BlockSpec((1, tq, D_ATT), lambda i, j: (i, j, 0)),
        compiler_params=_params(("parallel", "parallel")),
        name=f"diff_attention_{nseg}seg",
    )(*args)


def _hyena_kernel(pa_ref, cw_ref, cb_ref, cm_ref, sm_ref, kt_ref, bias_ref, o_ref,
                  p_scr, z_scr, zb_scr, pb_scr, qb_scr, *, seq, rb):
    row = lax.broadcasted_iota(jnp.int32, (seq, 1), 0)
    for j in range(IN_A // D_HY):
        cols = slice(j * D_HY, (j + 1) * D_HY)
        pa = pa_ref[0, :, cols]
        prev = jnp.where(row > 0, pltpu.roll(pa, 1, 0), 0.0)
        nxt = jnp.where(row < seq - 1, pltpu.roll(pa, seq - 1, 0), 0.0)
        p_scr[:, cols] = (prev * cw_ref[0:1, cols] + pa * cw_ref[1:2, cols] + nxt * cw_ref[2:3, cols]
                          + cb_ref[:, cols])

    z_scr[...] = p_scr[:, :D_HY]
    for o in range(HY_ORDER):
        zb_scr[...] = z_scr[...].astype(BF16)
        gate_cols = slice((o + 1) * D_HY, (o + 2) * D_HY)

        def spectrum(i, carry):
            rows = pl.ds(pl.multiple_of(i * rb, rb), rb)
            a = _bdot(cm_ref[rows, :], zb_scr[...])
            b = _bdot(sm_ref[rows, :], zb_scr[...])
            kre, kim = kt_ref[2 * o, rows, :], kt_ref[2 * o + 1, rows, :]
            pb_scr[rows, :] = (a * kre + b * kim).astype(BF16)
            qb_scr[rows, :] = (b * kre - a * kim).astype(BF16)
            return carry

        lax.fori_loop(0, seq // rb, spectrum, 0)

        def synth(i, carry):
            rows = pl.ds(pl.multiple_of(i * rb, rb), rb)
            y = _bdot(cm_ref[rows, :], pb_scr[...]) + _bdot(sm_ref[rows, :], qb_scr[...])
            z_scr[rows, :] = p_scr[rows, gate_cols] * (y + bias_ref[o:o + 1, :] * z_scr[rows, :])
            return carry

        lax.fori_loop(0, seq // rb, synth, 0)
    o_ref[0] = z_scr[...].astype(BF16)


def _hyena(pa, cw, cb, cm, sm, ktab, bias, *, rb):
    b, seq, _ = pa.shape
    return pl.pallas_call(
        functools.partial(_hyena_kernel, seq=seq, rb=rb),
        out_shape=jax.ShapeDtypeStruct((b, seq, D_HY), BF16),
        grid=(b,),
        in_specs=[pl.BlockSpec((1, seq, IN_A), lambda i: (i, 0, 0), pipeline_mode=pl.Buffered(1)),
                  _resident((3, IN_A)), _resident((1, IN_A)), _resident((seq, seq)), _resident((seq, seq)),
                  _resident((2 * HY_ORDER, seq, D_HY)), _resident((HY_ORDER, D_HY))],
        out_specs=pl.BlockSpec((1, seq, D_HY), lambda i: (i, 0, 0)),
        scratch_shapes=[pltpu.VMEM((seq, IN_A), F32), pltpu.VMEM((seq, D_HY), F32),
                        pltpu.VMEM((seq, D_HY), BF16), pltpu.VMEM((seq, D_HY), BF16),
                        pltpu.VMEM((seq, D_HY), BF16)],
        compiler_params=_params(("parallel",)),
        name=f"hyena_mixer_{seq}",
    )(pa, cw, cb, cm, sm, ktab, bias)


def _hyena_blocked_kernel(pa_ref, cw_ref, cb_ref, cm_ref, sm_ref, kt_ref, bias_ref, o_ref,
                          p_scr, z_scr, zb_scr, pb_scr, qb_scr, *, seq, lb, rb):
    row = lax.broadcasted_iota(jnp.int32, (seq, 1), 0)
    for j in range(IN_A // D_HY):
        cols = slice(j * D_HY, (j + 1) * D_HY)
        pa = pa_ref[0, :, cols]
        prev = jnp.where(row > 0, pltpu.roll(pa, 1, 0), 0.0)
        nxt = jnp.where(row < seq - 1, pltpu.roll(pa, seq - 1, 0), 0.0)
        p_scr[:, cols] = (prev * cw_ref[0:1, cols] + pa * cw_ref[1:2, cols] + nxt * cw_ref[2:3, cols]
                          + cb_ref[:, cols])

    per_block = lb // rb
    z_scr[...] = p_scr[:, :D_HY]
    for o in range(HY_ORDER):
        zb_scr[...] = z_scr[...].astype(BF16)
        gate_cols = slice((o + 1) * D_HY, (o + 2) * D_HY)

        def spectrum(i, carry):
            rows = pl.ds(pl.multiple_of(i * rb, rb), rb)
            cmr, smr = cm_ref[rows, :], sm_ref[rows, :]
            a0, b0 = _bdot(cmr, zb_scr[0:lb, :]), _bdot(smr, zb_scr[0:lb, :])
            a1, b1 = _bdot(cmr, zb_scr[lb:seq, :]), _bdot(smr, zb_scr[lb:seq, :])
            k0r, k0i = kt_ref[o, 0, 0, rows, :], kt_ref[o, 0, 1, rows, :]
            kpr, kpi = kt_ref[o, 1, 0, rows, :], kt_ref[o, 1, 1, rows, :]
            kmr, kmi = kt_ref[o, 2, 0, rows, :], kt_ref[o, 2, 1, rows, :]
            pb_scr[0, rows, :] = (a0 * k0r + b0 * k0i + a1 * kmr + b1 * kmi).astype(BF16)
            qb_scr[0, rows, :] = (b0 * k0r - a0 * k0i + b1 * kmr - a1 * kmi).astype(BF16)
            pb_scr[1, rows, :] = (a1 * k0r + b1 * k0i + a0 * kpr + b0 * kpi).astype(BF16)
            qb_scr[1, rows, :] = (b1 * k0r - a1 * k0i + b0 * kpr - a0 * kpi).astype(BF16)
            return carry

        lax.fori_loop(0, per_block, spectrum, 0)

        def synth(i, carry):
            blk = i // per_block
            rows_in = pl.ds(pl.multiple_of((i % per_block) * rb, rb), rb)
            rows_out = pl.ds(pl.multiple_of(i * rb, rb), rb)
            y = _bdot(cm_ref[rows_in, :], pb_scr[blk]) + _bdot(sm_ref[rows_in, :], qb_scr[blk])
            z_scr[rows_out, :] = p_scr[rows_out, gate_cols] * (y + bias_ref[o:o + 1, :] * z_scr[rows_out, :])
            return carry

        lax.fori_loop(0, seq // rb, synth, 0)
    o_ref[0] = z_scr[...].astype(BF16)


def _hyena_blocked(pa, cw, cb, cm, sm, ktab, bias, *, layer, lb, rb):
    b, seq, _ = pa.shape
    ktab_spec = pl.BlockSpec((None, HY_ORDER, 3, 2, lb, D_HY), lambda i: (layer, 0, 0, 0, 0, 0),
                             pipeline_mode=pl.Buffered(1))
    return pl.pallas_call(
        functools.partial(_hyena_blocked_kernel, seq=seq, lb=lb, rb=rb),
        out_shape=jax.ShapeDtypeStruct((b, seq, D_HY), BF16),
        grid=(b,),
        in_specs=[pl.BlockSpec((1, seq, IN_A), lambda i: (i, 0, 0), pipeline_mode=pl.Buffered(1)),
                  _resident((3, IN_A)), _resident((1, IN_A)), _resident((lb, lb)), _resident((lb, lb)),
                  ktab_spec, _resident((HY_ORDER, D_HY))],
        out_specs=pl.BlockSpec((1, seq, D_HY), lambda i: (i, 0, 0)),
        scratch_shapes=[pltpu.VMEM((seq, IN_A), F32), pltpu.VMEM((seq, D_HY), F32),
                        pltpu.VMEM((seq, D_HY), BF16), pltpu.VMEM((2, lb, D_HY), BF16),
                        pltpu.VMEM((2, lb, D_HY), BF16)],
        compiler_params=_params(("parallel",)),
        name=f"hyena_block_mixer_{seq}",
    )(pa, cw, cb, cm, sm, ktab, bias)


def _out_kernel(ohy_ref, osg_ref, oat_ref, x_ref, mod_ref, g2_ref, w_ref, nx_ref, h2_ref):
    d = D_MODEL
    mix = (_bdot(ohy_ref[...], w_ref[:D_HY]) + _bdot(osg_ref[...], w_ref[D_HY:D_HY + D_SG])
           + _bdot(oat_ref[...], w_ref[D_HY + D_SG:]))
    nx = x_ref[...] + mod_ref[0, :, 2 * d:3 * d] * mix
    nx_ref[...] = nx
    h2 = _rmsnorm_rows(nx, g2_ref[...]) * (1.0 + mod_ref[0, :, 4 * d:5 * d]) + mod_ref[0, :, 3 * d:4 * d]
    h2_ref[...] = h2.astype(BF16)


def _out_proj(ohy, osg, oat, xs, modl, g2, w_out_b, *, tm, mod_row):
    n, d = xs.shape
    row = lambda width: pl.BlockSpec((tm, width), lambda i: (i, 0))
    return pl.pallas_call(
        _out_kernel,
        out_shape=(jax.ShapeDtypeStruct((n, d), F32), jax.ShapeDtypeStruct((n, d), BF16)),
        grid=(n // tm,),
        in_specs=[row(D_HY), row(D_SG), row(D_ATT), row(d), _mod_spec(mod_row),
                  pl.BlockSpec((1, d), lambda i: (0, 0)), _resident((D_MIX, d))],
        out_specs=(row(d), row(d)),
        compiler_params=_params(("parallel",)),
        name="out_proj",
    )(ohy, osg, oat, xs, modl, g2, w_out_b)


def _ffn_kernel(h_ref, hp_ref, hn_ref, nx_ref, mod_ref, wu_ref, cw_ref, cb_ref, wd_ref, o_ref,
                hc_scr, ga_scr, va_scr, gb_scr, vb_scr, *, tm, seq):
    d = D_MODEL
    half = d // 2
    nchunk = D_FF // FF_CHUNK
    rows_ext = tm + 2 * HALO
    start = (pl.program_id(0) * tm) % seq
    hp, hn = hp_ref[...], hn_ref[...]
    hc_scr[0:HALO, :] = jnp.where(start == 0, jnp.zeros_like(hp), hp)
    hc_scr[HALO:HALO + tm, :] = h_ref[...]
    hc_scr[HALO + tm:rows_ext, :] = jnp.where(start + tm == seq, jnp.zeros_like(hn), hn)

    def chunk_cols(j):
        return pl.ds(pl.multiple_of(j * FF_CHUNK, FF_CHUNK), FF_CHUNK)

    def up(j, g_scr, v_scr):
        hc = hc_scr[...]
        g_scr[...] = _bdot(hc, wu_ref[:, chunk_cols(j)])
        v_scr[...] = _bdot(hc, wu_ref[:, chunk_cols(nchunk + j)])

    def conv(u_scr, j):
        u = u_scr[...]
        w = cw_ref[:, chunk_cols(j)]
        prev = pltpu.roll(u, 1, 0)[HALO:HALO + tm]
        nxt = pltpu.roll(u, rows_ext - 1, 0)[HALO:HALO + tm]
        return prev * w[0:1] + u[HALO:HALO + tm] * w[1:2] + nxt * w[2:3] + cb_ref[:, chunk_cols(j)]

    def activate(j, g_scr, v_scr):
        gate = conv(g_scr, j)
        val = conv(v_scr, nchunk + j)
        return (gate * jax.nn.sigmoid(gate) * val).astype(BF16)

    def down(j, act):
        o_ref[:, :half] += _bdot(act, wd_ref[j, :, :half])
        o_ref[:, half:] += _bdot(act, wd_ref[j, :, half:])

    o_ref[...] = jnp.zeros_like(o_ref)
    up(0, ga_scr, va_scr)

    def pair(i, carry):
        j = 2 * i
        act = activate(j, ga_scr, va_scr)
        up(j + 1, gb_scr, vb_scr)
        down(j, act)
        act = activate(j + 1, gb_scr, vb_scr)
        up(j + 2, ga_scr, va_scr)
        down(j + 1, act)
        return carry

    lax.fori_loop(0, (nchunk - 1) // 2, pair, 0)
    down(nchunk - 1, activate(nchunk - 1, ga_scr, va_scr))
    o_ref[...] = nx_ref[...] + mod_ref[0, :, 5 * d:6 * d] * o_ref[...]


def _ffn(h2, nx, modl, wu, cw, cb, wd, *, tm, seq, mod_row):
    n, d = nx.shape
    assert seq % tm == 0 and (D_FF // FF_CHUNK) % 2 == 1
    nh = n // HALO
    per = tm // HALO
    rows_ext = tm + 2 * HALO
    row = lambda width: pl.BlockSpec((tm, width), lambda i: (i, 0))
    return pl.pallas_call(
        functools.partial(_ffn_kernel, tm=tm, seq=seq),
        scratch_shapes=[pltpu.VMEM((rows_ext, d), BF16)] + [pltpu.VMEM((rows_ext, FF_CHUNK), F32)] * 4,
        out_shape=jax.ShapeDtypeStruct((n, d), F32),
        grid=(n // tm,),
        in_specs=[row(d),
                  pl.BlockSpec((HALO, d), lambda i: (jnp.maximum(i * per - 1, 0), 0)),
                  pl.BlockSpec((HALO, d), lambda i: (jnp.minimum((i + 1) * per, nh - 1), 0)),
                  row(d), _mod_spec(mod_row),
                  _resident(wu.shape), _resident(cw.shape), _resident(cb.shape), _resident(wd.shape)],
        out_specs=row(d),
        compiler_params=_params(("parallel",)),
        name="conv_ffn",
    )(h2, h2, h2, nx, modl, wu, cw, cb, wd)


def _pad_to(a, shape):
    return jnp.pad(a, [(0, t - s) for s, t in zip(a.shape, shape)])


def kernel(x, c, ctx, c_ctx, w_mod, b_mod, norm1_g, w_in, hy_conv_w, hy_conv_b, hy_w1, hy_b1, hy_w2, hy_b2,
           hy_w3, hy_freq, hy_bias, sg_ln_g, sg_ln_b, sg_w, sg_b, q_norm_g, k_norm_g, lam_q1, lam_k1, lam_q2,
           lam_k2, subln_g, w_out, norm2_g, ffn_w_up, ffn_conv_w, ffn_conv_b, ffn_w_down):
    bsz, seq, d = x.shape
    lc = ctx.shape[1]
    depth = w_mod.shape[0]
    assert d == D_MODEL and bsz + 1 <= MOD_ROWS and (bsz * lc) % 512 == 0 and seq % 512 == 0

    cc = _pad_to(jnp.concatenate([c, c_ctx[None, :]], axis=0), (MOD_ROWS, d))
    mod = _modulation(cc, w_mod, b_mod)
    ctx_row = bsz

    w1p = _pad_to(hy_w1, (depth, LANES, LANES))
    b1p = _pad_to(hy_b1, (depth, LANES)).reshape(depth, 1, LANES)
    frp = _pad_to(hy_freq, (depth, LANES)).reshape(depth, 1, LANES)
    w2p = _pad_to(hy_w2, (depth, LANES, LANES))
    b2p = _pad_to(hy_b2, (depth, LANES)).reshape(depth, 1, LANES)
    w3p = _pad_to(hy_w3, (depth, LANES, HY_ORDER * 2 * D_HY))
    lb = seq // 2
    dft = {}
    for s in (lb, lc):
        cmat, smat, _, _ = _dft_tables(s)
        dft[s] = (jnp.asarray(cmat).astype(BF16), jnp.asarray(smat).astype(BF16))
    ktab = _hyena_filters_blocked(seq, lb, w1p, b1p, frp, w2p, b2p, w3p, *dft[lb])
    nctx = depth - 1
    ktab_c = _hyena_filters(lc, w1p[:nctx], b1p[:nctx], frp[:nctx], w2p[:nctx], b2p[:nctx], w3p[:nctx], *dft[lc])

    tm = 512
    nchunk = D_FF // FF_CHUNK
    xs = x.reshape(bsz * seq, d)
    cs = ctx.reshape(bsz * lc, d)
    lat_row = lambda i: (i * tm) // seq
    ctx_mod = lambda i: ctx_row

    for l in range(depth):
        last = l == depth - 1
        lam_init = 0.8 - 0.6 * math.exp(-0.3 * l)
        modl = mod[l].reshape(MOD_ROWS, 1, 6 * d)
        g1 = norm1_g[l].reshape(1, d)
        g2 = norm2_g[l].reshape(1, d)
        w_in_b = w_in[l].astype(BF16)
        w_out_b = w_out[l].astype(BF16)
        wu = ffn_w_up[l].astype(BF16)
        wd = ffn_w_down[l].astype(BF16).reshape(nchunk, FF_CHUNK, d)
        fcw = ffn_conv_w[l]
        fcb = ffn_conv_b[l].reshape(1, 2 * D_FF)
        qn = jnp.tile(q_norm_g[l], LANES // QK_DIM).reshape(1, LANES)
        kn = jnp.tile(k_norm_g[l], LANES // QK_DIM).reshape(1, LANES)
        lng = sg_ln_g[l].reshape(1, D_SG)
        lnb = sg_ln_b[l].reshape(1, D_SG)
        wcat = sg_w[l].transpose(1, 0, 2).reshape(SG_CHUNK, SG_GROUPS * SG_CHUNK).astype(BF16)
        bstab = jnp.repeat(sg_b[l].T, D_SG // SG_GROUPS, axis=1)
        lamv = _pad_to(jnp.stack([lam_q1[l], lam_k1[l], lam_q2[l], lam_k2[l]]), (4, LANES))
        subg = subln_g[l].reshape(1, V_DIM)
        hcw = hy_conv_w[l]
        hcb = hy_conv_b[l].reshape(1, IN_A)
        sgu = (qn, kn, lng, lnb, wcat, bstab)

        pa, osg, q, k, v = _in_proj(xs, modl, g1, w_in_b, *sgu, seq=seq, tm=tm, use_rope=True, mod_row=lat_row)
        if last:
            kc, vc = _in_proj_kv(cs, modl, g1, w_in_b[:, IN_A + IN_B + D_QK:], kn, tm=tm, mod_row=ctx_mod)
        else:
            pa_c, osg_c, qc, kc, vc = _in_proj(cs, modl, g1, w_in_b, *sgu, seq=lc, tm=tm, use_rope=False,
                                               mod_row=ctx_mod)
        b3 = lambda a, t: a.reshape(bsz, t, a.shape[-1])
        kc3, vc3 = b3(kc, lc), b3(vc, lc)
        oat = _attention(lamv, b3(q, seq), [(b3(k, seq), b3(v, seq), ATT_KEY_PIECES), (kc3, vc3, 1)], subg,
                         tq=256, lam_init=lam_init)
        ohy = _hyena_blocked(b3(pa, seq), hcw, hcb, *dft[lb], ktab, hy_bias[l], layer=l, lb=lb, rb=512)
        nx, h2 = _out_proj(ohy.reshape(-1, D_HY), osg, oat.reshape(-1, D_ATT), xs, modl, g2, w_out_b,
                           tm=tm, mod_row=lat_row)
        xs_next = _ffn(h2, nx, modl, wu, fcw, fcb, wd, tm=tm, seq=seq, mod_row=lat_row)

        if not last:
            oat_c = _attention(lamv, b3(qc, lc), [(kc3, vc3, 1)], subg, tq=lc, lam_init=lam_init)
            ohy_c = _hyena(b3(pa_c, lc), hcw, hcb, *dft[lc], ktab_c[l], hy_bias[l], rb=lc)
            nx_c, h2_c = _out_proj(ohy_c.reshape(-1, D_HY), osg_c, oat_c.reshape(-1, D_ATT), cs, modl, g2,
                                   w_out_b, tm=tm, mod_row=ctx_mod)
            cs = _ffn(h2_c, nx_c, modl, wu, fcw, fcb, wd, tm=min(tm, lc), seq=lc, mod_row=ctx_mod)
        xs = xs_next
    return xs.reshape(bsz, seq, d)
```

```python
import functools
import math

import jax
import jax.numpy as jnp
import numpy as np
from jax import lax
from jax.experimental import pallas as pl
from jax.experimental.pallas import tpu as pltpu

F32 = jnp.float32
BF16 = jnp.bfloat16

D_MODEL = 1024
DEPTH = 2
GRID_W = 64
EPS = 1e-6
D_HY = 256
HY_ORDER = 2
HY_BANDS = 16
HY_EMB = 1 + 2 * HY_BANDS
HY_FFN = 64
HY_DECAY_TARGET = 1e-2
HY_FAST_DECAY = 0.3
HY_SLOW_DECAY = 1.5
SG_GROUPS = 4
D_SG = 256
SG_CHUNK = 128
ATT_HEADS = 4
QK_DIM = 64
V_DIM = 128
D_ATT = ATT_HEADS * V_DIM
ROPE_PAIRS = QK_DIM // 4
ROPE_BASE = 10000.0
D_MIX = D_HY + D_SG + D_ATT
IN_A = 3 * D_HY
IN_B = 2 * D_SG
D_QK = 2 * ATT_HEADS * QK_DIM
IN_C = 2 * D_QK + D_ATT
D_IN = IN_A + IN_B + IN_C
D_FF = 2816
D_VAUG = 2 * D_ATT
LOG2E = 1.4426950408889634

LANES = 128
BF16_SUBLANES = 16
VMEM_LIMIT_BYTES = 56 * 1024 * 1024

MOD_ROWS = 16
FF_CHUNK = 256
ATT_KEY_PIECES = 1
HALO = BF16_SUBLANES


def _params(sem, vmem=VMEM_LIMIT_BYTES):
    return pltpu.CompilerParams(dimension_semantics=sem, vmem_limit_bytes=vmem)


def _resident(shape):
    nd = len(shape)
    return pl.BlockSpec(shape, lambda *_: (0,) * nd, pipeline_mode=pl.Buffered(1))


@functools.lru_cache(maxsize=None)
def _dft_tables(seq):
    n2 = 2 * seq
    f = np.arange(seq, dtype=np.int64)
    m = ((2 * f[:, None] + 1) * (2 * f[None, :] + 1)) % (4 * n2)
    ang = (2.0 * np.pi / (4 * n2)) * m.astype(np.float64)
    half = np.pi * (2 * f + 1) / (2.0 * n2)
    return (np.cos(ang).astype(np.float32), np.sin(ang).astype(np.float32),
            np.cos(half).astype(np.float32)[:, None], np.sin(half).astype(np.float32)[:, None])


@functools.lru_cache(maxsize=None)
def _filter_tables(seq):
    t = np.linspace(0.0, 1.0, seq, dtype=np.float32)[:, None]
    t_r = np.arange(seq, dtype=np.float32)[:, None]
    bands = np.linspace(1e-4, HY_BANDS - 1, HY_BANDS, dtype=np.float32)[None, :]
    w = (2.0 * math.pi * t_r / seq).astype(np.float32)
    z = np.concatenate([t, np.cos(bands * w), -np.sin(bands * w)], axis=-1).astype(np.float32)
    zp = np.zeros((seq, LANES), np.float32)
    zp[:, :HY_EMB] = z
    min_decay = math.log(HY_DECAY_TARGET) / HY_SLOW_DECAY
    max_decay = math.log(HY_DECAY_TARGET) / HY_FAST_DECAY
    deltas = np.abs(np.linspace(min_decay, max_decay, D_HY, dtype=np.float32))
    decay = np.exp(-t * deltas[None, :]).astype(np.float32)
    return zp, decay


@functools.lru_cache(maxsize=None)
def _filter_tables_reversed(seq, lb):
    zp, decay = _filter_tables(seq)
    idx = lb - np.arange(lb)
    return np.ascontiguousarray(zp[idx]), np.ascontiguousarray(decay[idx])


@functools.lru_cache(maxsize=None)
def _rope_tables(seq):
    pos = np.arange(seq)
    row = (pos // GRID_W).astype(np.float32)
    col = (pos % GRID_W).astype(np.float32)
    inv = (ROPE_BASE ** (-np.arange(ROPE_PAIRS, dtype=np.float32) / ROPE_PAIRS)).astype(np.float32)
    lane = np.arange(LANES)
    axis = (lane % QK_DIM) // (2 * ROPE_PAIRS)
    half = (lane % (2 * ROPE_PAIRS)) // ROPE_PAIRS
    pair = lane % ROPE_PAIRS
    p = np.where(axis[None, :] == 0, row[:, None], col[:, None]).astype(np.float32)
    ang = (p * inv[pair][None, :]).astype(np.float32)
    cos, sin = np.cos(ang).astype(np.float32), np.sin(ang).astype(np.float32)
    sin_lo = np.where(half[None, :] == 0, -sin, 0.0).astype(np.float32)
    sin_hi = np.where(half[None, :] == 1, sin, 0.0).astype(np.float32)
    return cos, sin_lo, sin_hi


def _group_sum(x):
    lane = lax.broadcasted_iota(jnp.int32, (1, LANES), 1)
    lo = lane < QK_DIM
    s_lo = jnp.sum(jnp.where(lo, x, 0.0), axis=-1, keepdims=True)
    s_hi = jnp.sum(jnp.where(lo, 0.0, x), axis=-1, keepdims=True)
    return jnp.where(lo, s_lo, s_hi)


def _rmsnorm_rows(x, g):
    return x * lax.rsqrt(jnp.mean(x * x, axis=-1, keepdims=True) + EPS) * g


def _bdot(a, b):
    return jnp.dot(a, b, preferred_element_type=F32)


def _split_bf16(x):
    hi = x.astype(BF16)
    return hi, (x - hi.astype(F32)).astype(BF16)


def _dot3(a, b):
    a_hi, a_lo = _split_bf16(a)
    b_hi, b_lo = _split_bf16(b)
    return _bdot(a_hi, b_hi) + (_bdot(a_lo, b_hi) + _bdot(a_hi, b_lo))


def _v_aug(v):
    ones = jnp.ones((v.shape[0], V_DIM), BF16)
    vb = v.astype(BF16)
    parts = []
    for h in range(ATT_HEADS):
        parts += [vb[:, h * V_DIM:(h + 1) * V_DIM], ones]
    return jnp.concatenate(parts, axis=-1)


def _mod_kernel(cc_ref, w_ref, b_ref, o_ref):
    cc = cc_ref[...]
    a = cc * jax.nn.sigmoid(cc)
    o_ref[0] = _dot3(a, w_ref[0]) + b_ref[0]


def _modulation(cc, w_mod, b_mod):
    depth, d, n = w_mod.shape
    tn = 768
    return pl.pallas_call(
        _mod_kernel,
        out_shape=jax.ShapeDtypeStruct((depth, MOD_ROWS, n), F32),
        grid=(depth, n // tn),
        in_specs=[pl.BlockSpec((MOD_ROWS, d), lambda l, j: (0, 0)),
                  pl.BlockSpec((1, d, tn), lambda l, j: (l, 0, j)),
                  pl.BlockSpec((1, 1, tn), lambda l, j: (l, 0, j))],
        out_specs=pl.BlockSpec((1, MOD_ROWS, tn), lambda l, j: (l, 0, j)),
        compiler_params=_params(("parallel", "parallel")),
        name="adaln_modulation",
    )(cc, w_mod, b_mod.reshape(depth, 1, n))


def _filter_kernel(z_ref, w1_ref, b1_ref, fr_ref, w2_ref, b2_ref, w3_ref, dec_ref, cm_ref, sm_ref,
                   ch_ref, sh_ref, o_ref, *, seq):
    fr = fr_ref[0]
    h = jnp.sin(fr * (_dot3(z_ref[...], w1_ref[0]) + b1_ref[0]))
    h = jnp.sin(fr * (_dot3(h, w2_ref[0]) + b2_ref[0]))
    h = _dot3(h, w3_ref[0])
    dec = dec_ref[...]
    hf = h[:, :D_HY] * dec
    row = lax.broadcasted_iota(jnp.int32, (seq, 1), 0)
    hb = jnp.where(row > 0, h[:, D_HY:] * dec, 0.0)
    gp = (hf + hb).astype(BF16)
    gm = (hf - hb).astype(BF16)
    cm, sm = cm_ref[...], sm_ref[...]
    ch, sh = ch_ref[...], sh_ref[...]
    scale = 1.0 / seq
    o_ref[0, 0] = (_bdot(cm, gp) * ch + _bdot(sm, gp) * sh) * scale
    o_ref[0, 1] = (_bdot(cm, gm) * sh - _bdot(sm, gm) * ch) * scale


def _hyena_filters(seq, w1p, b1p, frp, w2p, b2p, w3, cm, sm):
    layers = w1p.shape[0]
    zp, decay = _filter_tables(seq)
    _, _, ch, sh = _dft_tables(seq)
    out = pl.pallas_call(
        functools.partial(_filter_kernel, seq=seq),
        out_shape=jax.ShapeDtypeStruct((layers * HY_ORDER, 2, seq, D_HY), F32),
        grid=(layers, HY_ORDER),
        in_specs=[_resident((seq, LANES)),
                  pl.BlockSpec((1, LANES, LANES), lambda l, o: (l, 0, 0)),
                  pl.BlockSpec((1, 1, LANES), lambda l, o: (l, 0, 0)),
                  pl.BlockSpec((1, 1, LANES), lambda l, o: (l, 0, 0)),
                  pl.BlockSpec((1, LANES, LANES), lambda l, o: (l, 0, 0)),
                  pl.BlockSpec((1, 1, LANES), lambda l, o: (l, 0, 0)),
                  pl.BlockSpec((1, LANES, 2 * D_HY), lambda l, o: (l, 0, o)),
                  _resident((seq, D_HY)), _resident((seq, seq)), _resident((seq, seq)),
                  _resident((seq, 1)), _resident((seq, 1))],
        out_specs=pl.BlockSpec((1, 2, seq, D_HY), lambda l, o: (l * HY_ORDER + o, 0, 0, 0)),
        compiler_params=_params(("parallel", "parallel")),
        name=f"hyena_filter_spectra_{seq}",
    )(jnp.asarray(zp), w1p, b1p, frp, w2p, b2p, w3, jnp.asarray(decay), cm, sm, jnp.asarray(ch), jnp.asarray(sh))
    return out.reshape(layers, 2 * HY_ORDER, seq, D_HY)


def _filter_blocked_kernel(z_ref, zr_ref, w1_ref, b1_ref, fr_ref, w2_ref, b2_ref, w3_ref, dec_ref, decr_ref,
                           cm_ref, sm_ref, ch_ref, sh_ref, o_ref, hid_scr, hidr_scr, *, lb):
    fr = fr_ref[0]

    def hidden(z):
        h = jnp.sin(fr * (_dot3(z, w1_ref[0]) + b1_ref[0]))
        return jnp.sin(fr * (_dot3(h, w2_ref[0]) + b2_ref[0]))

    @pl.when(pl.program_id(1) == 0)
    def _():
        hid_scr[...] = hidden(z_ref[...])
        hidr_scr[...] = hidden(zr_ref[...])

    hall = _dot3(jnp.concatenate([hid_scr[...], hidr_scr[...]], axis=0), w3_ref[0])
    hn, hr = hall[:2 * lb], hall[2 * lb:]
    dec, decr = dec_ref[...], decr_ref[...]
    hf, hb = hn[:, :D_HY] * dec, hn[:, D_HY:] * dec
    hfr, hbr = hr[:, :D_HY] * decr, hr[:, D_HY:] * decr
    taps = ((hf[:lb], hb[:lb]),
            (hf[lb:], hfr),
            (hbr, hb[lb:]))
    row = lax.broadcasted_iota(jnp.int32, (lb, 1), 0)
    cm, sm = cm_ref[...], sm_ref[...]
    ch, sh = ch_ref[...], sh_ref[...]
    scale = 1.0 / lb
    for d, (fwd, bwd) in enumerate(taps):
        bwd = jnp.where(row > 0, bwd, 0.0)
        gp = (fwd + bwd).astype(BF16)
        gm = (fwd - bwd).astype(BF16)
        o_ref[0, d, 0] = (_bdot(cm, gp) * ch + _bdot(sm, gp) * sh) * scale
        o_ref[0, d, 1] = (_bdot(cm, gm) * sh - _bdot(sm, gm) * ch) * scale


def _hyena_filters_blocked(seq, lb, w1p, b1p, frp, w2p, b2p, w3, cm, sm):
    assert seq == 2 * lb
    layers = w1p.shape[0]
    zp, decay = _filter_tables(seq)
    zr, decr = _filter_tables_reversed(seq, lb)
    _, _, ch, sh = _dft_tables(lb)
    per_layer = lambda l, o: (l, 0, 0)
    out = pl.pallas_call(
        functools.partial(_filter_blocked_kernel, lb=lb),
        out_shape=jax.ShapeDtypeStruct((layers * HY_ORDER, 3, 2, lb, D_HY), F32),
        grid=(layers, HY_ORDER),
        in_specs=[_resident((seq, LANES)), _resident((lb, LANES)),
                  pl.BlockSpec((1, LANES, LANES), per_layer), pl.BlockSpec((1, 1, LANES), per_layer),
                  pl.BlockSpec((1, 1, LANES), per_layer), pl.BlockSpec((1, LANES, LANES), per_layer),
                  pl.BlockSpec((1, 1, LANES), per_layer),
                  pl.BlockSpec((1, LANES, 2 * D_HY), lambda l, o: (l, 0, o)),
                  _resident((seq, D_HY)), _resident((lb, D_HY)), _resident((lb, lb)), _resident((lb, lb)),
                  _resident((lb, 1)), _resident((lb, 1))],
        out_specs=pl.BlockSpec((1, 3, 2, lb, D_HY), lambda l, o: (l * HY_ORDER + o, 0, 0, 0, 0)),
        scratch_shapes=[pltpu.VMEM((seq, LANES), F32), pltpu.VMEM((lb, LANES), F32)],
        compiler_params=_params(("arbitrary", "arbitrary")),
        name=f"hyena_block_filter_spectra_{seq}",
    )(jnp.asarray(zp), jnp.asarray(zr), w1p, b1p, frp, w2p, b2p, w3, jnp.asarray(decay), jnp.asarray(decr),
      cm, sm, jnp.asarray(ch), jnp.asarray(sh))
    return out.reshape(layers, HY_ORDER, 3, 2, lb, D_HY)


def _prenorm(x_ref, mod_ref, g_ref, shift_col, scale_col):
    d = D_MODEL
    shift = mod_ref[0, :, shift_col * d:(shift_col + 1) * d]
    scale = mod_ref[0, :, scale_col * d:(scale_col + 1) * d]
    return (_rmsnorm_rows(x_ref[...], g_ref[...]) * (1.0 + scale) + shift).astype(BF16)


def _qk_head_blocks(p, gain, rope):
    out = []
    for h in range(ATT_HEADS):
        xb = p[:, h * LANES:(h + 1) * LANES]
        xb = xb * lax.rsqrt(_group_sum(xb * xb) * (1.0 / QK_DIM) + EPS) * gain
        if rope is not None:
            cos, sin_lo, sin_hi = rope
            xb = (xb * cos + pltpu.roll(xb, LANES - ROPE_PAIRS, 1) * sin_lo
                  + pltpu.roll(xb, ROPE_PAIRS, 1) * sin_hi)
        out.append(xb)
    return jnp.concatenate(out, axis=-1)


def _in_kernel(*refs, tm, use_rope):
    if use_rope:
        (x_ref, mod_ref, g_ref, wf_ref, cos_ref, slo_ref, shi_ref, qn_ref, kn_ref, lng_ref, lnb_ref,
         wcat_ref, bst_ref, pa_ref, osg_ref, q_ref, k_ref, v_ref, w_ref) = refs
        rope = (cos_ref[...], slo_ref[...], shi_ref[...])
    else:
        (x_ref, mod_ref, g_ref, wf_ref, qn_ref, kn_ref, lng_ref, lnb_ref,
         wcat_ref, bst_ref, pa_ref, osg_ref, q_ref, k_ref, v_ref, w_ref) = refs
        rope = None

    @pl.when(pl.program_id(0) == 0)
    def _():
        w_ref[...] = wf_ref[...].astype(BF16)

    hb = _prenorm(x_ref, mod_ref, g_ref, 0, 1)

    c0 = IN_A + IN_B
    pb = _bdot(hb, w_ref[:, IN_A:c0])
    pq = _bdot(hb, w_ref[:, c0:c0 + D_QK])
    pk = _bdot(hb, w_ref[:, c0 + D_QK:c0 + 2 * D_QK])

    pb = jax.nn.gelu(pb)
    u = pb[:, :D_SG]
    vn = []
    for j in range(D_SG // LANES):
        xb = pb[:, D_SG + j * LANES:D_SG + (j + 1) * LANES]
        xc = xb - _group_sum(xb) * (1.0 / QK_DIM)
        vn.append(xc * lax.rsqrt(_group_sum(xc * xc) * (1.0 / QK_DIM) + EPS))
    vn = (jnp.concatenate(vn, axis=-1) * lng_ref[...] + lnb_ref[...]).astype(BF16)
    lane = lax.broadcasted_iota(jnp.int32, (1, D_SG), 1)
    grp = lane // (D_SG // SG_GROUPS)
    zero = jnp.zeros((SG_CHUNK, D_SG), BF16)
    for ci in range(tm // SG_CHUNK):
        rows = slice(ci * SG_CHUNK, (ci + 1) * SG_CHUNK)
        vc = vn[rows]
        stacked = jnp.concatenate([jnp.where(grp == g, vc, zero) for g in range(SG_GROUPS)], axis=0)
        s = _bdot(wcat_ref[...], stacked) + bst_ref[...]
        osg_ref[rows, :] = (u[rows] * s).astype(BF16)

    q = _qk_head_blocks(pq, qn_ref[...], rope)
    q_ref[...] = (q * (QK_DIM ** -0.5 * LOG2E)).astype(BF16)
    pa_ref[...] = _bdot(hb, w_ref[:, :IN_A])
    k_ref[...] = _qk_head_blocks(pk, kn_ref[...], rope).astype(BF16)
    v_ref[...] = _v_aug(_bdot(hb, w_ref[:, c0 + 2 * D_QK:]))


def _in_kv_kernel(x_ref, mod_ref, g_ref, w_ref, kn_ref, k_ref, v_ref):
    hb = _prenorm(x_ref, mod_ref, g_ref, 0, 1)
    k = _qk_head_blocks(_bdot(hb, w_ref[:, :D_QK]), kn_ref[...], None)
    k_ref[...] = k.astype(BF16)
    v_ref[...] = _v_aug(_bdot(hb, w_ref[:, D_QK:]))


def _mod_spec(mod_row):
    return pl.BlockSpec((1, 1, 6 * D_MODEL), lambda i: (mod_row(i), 0, 0))


def _in_proj(xs, modl, g1, w_in_f32, qn, kn, lng, lnb, wcat, bstab, *, seq, tm, use_rope, mod_row):
    n, d = xs.shape
    row = lambda width: pl.BlockSpec((tm, width), lambda i: (i, 0))
    vec = lambda width: pl.BlockSpec((1, width), lambda i: (0, 0))
    in_specs = [row(d), _mod_spec(mod_row), vec(d), _resident((d, D_IN))]
    args = [xs, modl, g1, w_in_f32]
    if use_rope:
        nblk = seq // tm
        in_specs += [pl.BlockSpec((tm, LANES), lambda i: (i % nblk, 0))] * 3
        args += [jnp.asarray(t) for t in _rope_tables(seq)]
    in_specs += [vec(LANES), vec(LANES), vec(D_SG), vec(D_SG), _resident((SG_CHUNK, SG_GROUPS * SG_CHUNK)),
                 _resident((SG_CHUNK, D_SG))]
    args += [qn, kn, lng, lnb, wcat, bstab]
    return pl.pallas_call(
        functools.partial(_in_kernel, tm=tm, use_rope=use_rope),
        out_shape=(jax.ShapeDtypeStruct((n, IN_A), F32), jax.ShapeDtypeStruct((n, D_SG), BF16),
                   jax.ShapeDtypeStruct((n, D_QK), BF16), jax.ShapeDtypeStruct((n, D_QK), BF16),
                   jax.ShapeDtypeStruct((n, D_VAUG), BF16)),
        grid=(n // tm,),
        in_specs=in_specs,
        out_specs=(row(IN_A), row(D_SG), row(D_QK), row(D_QK), row(D_VAUG)),
        scratch_shapes=[pltpu.VMEM((d, D_IN), BF16)],
        compiler_params=_params(("arbitrary",)),
        name="in_proj_rope" if use_rope else "in_proj",
    )(*args)


def _in_proj_kv(xs, modl, g1, w_kv_b, kn, *, tm, mod_row):
    n, d = xs.shape
    row = lambda width: pl.BlockSpec((tm, width), lambda i: (i, 0))
    vec = lambda width: pl.BlockSpec((1, width), lambda i: (0, 0))
    return pl.pallas_call(
        _in_kv_kernel,
        out_shape=(jax.ShapeDtypeStruct((n, D_QK), BF16), jax.ShapeDtypeStruct((n, D_VAUG), BF16)),
        grid=(n // tm,),
        in_specs=[row(d), _mod_spec(mod_row), vec(d), _resident((d, D_QK + D_ATT)), vec(LANES)],
        out_specs=(row(D_QK), row(D_VAUG)),
        compiler_params=_params(("parallel",)),
        name="in_proj_kv",
    )(xs, modl, g1, w_kv_b, kn)


def _att_kernel(*refs, segments, lam_init, tq):
    lam_ref, q_ref = refs[0], refs[1]
    kv_refs = refs[2:2 + 2 * segments]
    g_ref, o_ref = refs[2 + 2 * segments], refs[3 + 2 * segments]
    lv = lam_ref[...]
    lam = (jnp.exp(jnp.sum(lv[0:1] * lv[1:2], axis=-1, keepdims=True))
           - jnp.exp(jnp.sum(lv[2:3] * lv[3:4], axis=-1, keepdims=True)) + lam_init)
    lo = lax.broadcasted_iota(jnp.int32, (1, LANES), 1) < QK_DIM
    nt = (((1,), (1,)), ((), ()))

    def head_scores(h):
        cols = slice(h * LANES, (h + 1) * LANES)
        qh = q_ref[0, :, cols]
        zero = jnp.zeros_like(qh)
        return [[lax.dot_general(qm, kv_refs[2 * i][0, :, cols], nt, preferred_element_type=F32)
                 for i in range(segments)]
                for qm in (jnp.where(lo, qh, zero), jnp.where(lo, zero, qh))]

    nxt = head_scores(0)
    for h in range(ATT_HEADS):
        cols = slice(h * LANES, (h + 1) * LANES)
        scores, nxt = nxt, (head_scores(h + 1) if h + 1 < ATT_HEADS else None)
        o = []
        for s in scores:
            m = jnp.max(s[0], axis=-1, keepdims=True)
            for si in s[1:]:
                m = jnp.maximum(m, jnp.max(si, axis=-1, keepdims=True))
            acc = None
            for i in range(segments):
                p = jnp.exp2((s[i] - m).astype(BF16))
                a_i = _bdot(p, kv_refs[2 * i + 1][0, :, 2 * h * V_DIM:2 * (h + 1) * V_DIM])
                acc = a_i if acc is None else acc + a_i
            o.append(acc[:, :V_DIM] / acc[:, V_DIM:])
        out = o[0] - lam * o[1]
        o_ref[0, :, cols] = (_rmsnorm_rows(out, g_ref[...]) * (1.0 - lam_init)).astype(BF16)


def _attention(lamv, q, kvs, subg, *, tq, lam_init):
    b, t, _ = q.shape
    in_specs = [pl.BlockSpec((4, LANES), lambda i, j: (0, 0)),
                pl.BlockSpec((1, tq, D_QK), lambda i, j: (i, j, 0))]
    args = [lamv, q]
    for k, v, pieces in kvs:
        tk = k.shape[1] // pieces
        for piece in range(pieces):
            in_specs += [pl.BlockSpec((1, tk, D_QK), lambda i, j, piece=piece: (i, piece, 0)),
                         pl.BlockSpec((1, tk, D_VAUG), lambda i, j, piece=piece: (i, piece, 0))]
            args += [k, v]
    in_specs.append(pl.BlockSpec((1, V_DIM), lambda i, j: (0, 0)))
    args.append(subg)
    nseg = (len(in_specs) - 3) // 2
    return pl.pallas_call(
        functools.partial(_att_kernel, segments=nseg, lam_init=lam_init, tq=tq),
        out_shape=jax.ShapeDtypeStruct((b, t, D_ATT), BF16),
        grid=(b, t // tq),
        in_specs=in_specs,
        out_specs=pl.BlockSpec((1, tq, D_ATT), lambda i, j: (i, j, 0)),
        compiler_params=_params(("parallel", "parallel")),
        name=f"diff_attention_{nseg}seg",
    )(*args)


def _hyena_kernel(pa_ref, cw_ref, cb_ref, cm_ref, sm_ref, kt_ref, bias_ref, o_ref,
                  p_scr, z_scr, zb_scr, pb_scr, qb_scr, *, seq, rb):
    row = lax.broadcasted_iota(jnp.int32, (seq, 1), 0)
    for j in range(IN_A // D_HY):
        cols = slice(j * D_HY, (j + 1) * D_HY)
        pa = pa_ref[0, :, cols]
        prev = jnp.where(row > 0, pltpu.roll(pa, 1, 0), 0.0)
        nxt = jnp.where(row < seq - 1, pltpu.roll(pa, seq - 1, 0), 0.0)
        p_scr[:, cols] = (prev * cw_ref[0:1, cols] + pa * cw_ref[1:2, cols] + nxt * cw_ref[2:3, cols]
                          + cb_ref[:, cols])

    z_scr[...] = p_scr[:, :D_HY]
    for o in range(HY_ORDER):
        zb_scr[...] = z_scr[...].astype(BF16)
        gate_cols = slice((o + 1) * D_HY, (o + 2) * D_HY)

        def spectrum(i, carry):
            rows = pl.ds(pl.multiple_of(i * rb, rb), rb)
            a = _bdot(cm_ref[rows, :], zb_scr[...])
            b = _bdot(sm_ref[rows, :], zb_scr[...])
            kre, kim = kt_ref[2 * o, rows, :], kt_ref[2 * o + 1, rows, :]
            pb_scr[rows, :] = (a * kre + b * kim).astype(BF16)
            qb_scr[rows, :] = (b * kre - a * kim).astype(BF16)
            return carry

        lax.fori_loop(0, seq // rb, spectrum, 0)

        def synth(i, carry):
            rows = pl.ds(pl.multiple_of(i * rb, rb), rb)
            y = _bdot(cm_ref[rows, :], pb_scr[...]) + _bdot(sm_ref[rows, :], qb_scr[...])
            z_scr[rows, :] = p_scr[rows, gate_cols] * (y + bias_ref[o:o + 1, :] * z_scr[rows, :])
            return carry

        lax.fori_loop(0, seq // rb, synth, 0)
    o_ref[0] = z_scr[...].astype(BF16)


def _hyena(pa, cw, cb, cm, sm, ktab, bias, *, rb):
    b, seq, _ = pa.shape
    return pl.pallas_call(
        functools.partial(_hyena_kernel, seq=seq, rb=rb),
        out_shape=jax.ShapeDtypeStruct((b, seq, D_HY), BF16),
        grid=(b,),
        in_specs=[pl.BlockSpec((1, seq, IN_A), lambda i: (i, 0, 0), pipeline_mode=pl.Buffered(1)),
                  _resident((3, IN_A)), _resident((1, IN_A)), _resident((seq, seq)), _resident((seq, seq)),
                  _resident((2 * HY_ORDER, seq, D_HY)), _resident((HY_ORDER, D_HY))],
        out_specs=pl.BlockSpec((1, seq, D_HY), lambda i: (i, 0, 0)),
        scratch_shapes=[pltpu.VMEM((seq, IN_A), F32), pltpu.VMEM((seq, D_HY), F32),
                        pltpu.VMEM((seq, D_HY), BF16), pltpu.VMEM((seq, D_HY), BF16),
                        pltpu.VMEM((seq, D_HY), BF16)],
        compiler_params=_params(("parallel",)),
        name=f"hyena_mixer_{seq}",
    )(pa, cw, cb, cm, sm, ktab, bias)


def _hyena_blocked_kernel(pa_ref, cw_ref, cb_ref, cm_ref, sm_ref, kt_ref, bias_ref, o_ref,
                          p_scr, z_scr, zb_scr, pb_scr, qb_scr, *, seq, lb, rb):
    row = lax.broadcasted_iota(jnp.int32, (seq, 1), 0)
    for j in range(IN_A // D_HY):
        cols = slice(j * D_HY, (j + 1) * D_HY)
        pa = pa_ref[0, :, cols]
        prev = jnp.where(row > 0, pltpu.roll(pa, 1, 0), 0.0)
        nxt = jnp.where(row < seq - 1, pltpu.roll(pa, seq - 1, 0), 0.0)
        p_scr[:, cols] = (prev * cw_ref[0:1, cols] + pa * cw_ref[1:2, cols] + nxt * cw_ref[2:3, cols]
                          + cb_ref[:, cols])

    per_block = lb // rb
    z_scr[...] = p_scr[:, :D_HY]
    for o in range(HY_ORDER):
        zb_scr[...] = z_scr[...].astype(BF16)
        gate_cols = slice((o + 1) * D_HY, (o + 2) * D_HY)

        def spectrum(i, carry):
            rows = pl.ds(pl.multiple_of(i * rb, rb), rb)
            cmr, smr = cm_ref[rows, :], sm_ref[rows, :]
            a0, b0 = _bdot(cmr, zb_scr[0:lb, :]), _bdot(smr, zb_scr[0:lb, :])
            a1, b1 = _bdot(cmr, zb_scr[lb:seq, :]), _bdot(smr, zb_scr[lb:seq, :])
            k0r, k0i = kt_ref[o, 0, 0, rows, :], kt_ref[o, 0, 1, rows, :]
            kpr, kpi = kt_ref[o, 1, 0, rows, :], kt_ref[o, 1, 1, rows, :]
            kmr, kmi = kt_ref[o, 2, 0, rows, :], kt_ref[o, 2, 1, rows, :]
            pb_scr[0, rows, :] = (a0 * k0r + b0 * k0i + a1 * kmr + b1 * kmi).astype(BF16)
            qb_scr[0, rows, :] = (b0 * k0r - a0 * k0i + b1 * kmr - a1 * kmi).astype(BF16)
            pb_scr[1, rows, :] = (a1 * k0r + b1 * k0i + a0 * kpr + b0 * kpi).astype(BF16)
            qb_scr[1, rows, :] = (b1 * k0r - a1 * k0i + b0 * kpr - a0 * kpi).astype(BF16)
            return carry

        lax.fori_loop(0, per_block, spectrum, 0)

        def synth(i, carry):
            blk = i // per_block
            rows_in = pl.ds(pl.multiple_of((i % per_block) * rb, rb), rb)
            rows_out = pl.ds(pl.multiple_of(i * rb, rb), rb)
            y = _bdot(cm_ref[rows_in, :], pb_scr[blk]) + _bdot(sm_ref[rows_in, :], qb_scr[blk])
            z_scr[rows_out, :] = p_scr[rows_out, gate_cols] * (y + bias_ref[o:o + 1, :] * z_scr[rows_out, :])
            return carry

        lax.fori_loop(0, seq // rb, synth, 0)
    o_ref[0] = z_scr[...].astype(BF16)


def _hyena_blocked(pa, cw, cb, cm, sm, ktab, bias, *, layer, lb, rb):
    b, seq, _ = pa.shape
    ktab_spec = pl.BlockSpec((None, HY_ORDER, 3, 2, lb, D_HY), lambda i: (layer, 0, 0, 0, 0, 0),
                             pipeline_mode=pl.Buffered(1))
    return pl.pallas_call(
        functools.partial(_hyena_blocked_kernel, seq=seq, lb=lb, rb=rb),
        out_shape=jax.ShapeDtypeStruct((b, seq, D_HY), BF16),
        grid=(b,),
        in_specs=[pl.BlockSpec((1, seq, IN_A), lambda i: (i, 0, 0), pipeline_mode=pl.Buffered(1)),
                  _resident((3, IN_A)), _resident((1, IN_A)), _resident((lb, lb)), _resident((lb, lb)),
                  ktab_spec, _resident((HY_ORDER, D_HY))],
        out_specs=pl.BlockSpec((1, seq, D_HY), lambda i: (i, 0, 0)),
        scratch_shapes=[pltpu.VMEM((seq, IN_A), F32), pltpu.VMEM((seq, D_HY), F32),
                        pltpu.VMEM((seq, D_HY), BF16), pltpu.VMEM((2, lb, D_HY), BF16),
                        pltpu.VMEM((2, lb, D_HY), BF16)],
        compiler_params=_params(("parallel",)),
        name=f"hyena_block_mixer_{seq}",
    )(pa, cw, cb, cm, sm, ktab, bias)


def _out_kernel(ohy_ref, osg_ref, oat_ref, x_ref, mod_ref, g2_ref, w_ref, nx_ref, h2_ref):
    d = D_MODEL
    mix = (_bdot(ohy_ref[...], w_ref[:D_HY]) + _bdot(osg_ref[...], w_ref[D_HY:D_HY + D_SG])
           + _bdot(oat_ref[...], w_ref[D_HY + D_SG:]))
    nx = x_ref[...] + mod_ref[0, :, 2 * d:3 * d] * mix
    nx_ref[...] = nx
    h2 = _rmsnorm_rows(nx, g2_ref[...]) * (1.0 + mod_ref[0, :, 4 * d:5 * d]) + mod_ref[0, :, 3 * d:4 * d]
    h2_ref[...] = h2.astype(BF16)


def _out_proj(ohy, osg, oat, xs, modl, g2, w_out_b, *, tm, mod_row):
    n, d = xs.shape
    row = lambda width: pl.BlockSpec((tm, width), lambda i: (i, 0))
    return pl.pallas_call(
        _out_kernel,
        out_shape=(jax.ShapeDtypeStruct((n, d), F32), jax.ShapeDtypeStruct((n, d), BF16)),
        grid=(n // tm,),
        in_specs=[row(D_HY), row(D_SG), row(D_ATT), row(d), _mod_spec(mod_row),
                  pl.BlockSpec((1, d), lambda i: (0, 0)), _resident((D_MIX, d))],
        out_specs=(row(d), row(d)),
        compiler_params=_params(("parallel",)),
        name="out_proj",
    )(ohy, osg, oat, xs, modl, g2, w_out_b)


def _ffn_kernel(h_ref, hp_ref, hn_ref, nx_ref, mod_ref, wu_ref, cw_ref, cb_ref, wd_ref, o_ref,
                hc_scr, ga_scr, va_scr, gb_scr, vb_scr, *, tm, seq):
    d = D_MODEL
    half = d // 2
    nchunk = D_FF // FF_CHUNK
    rows_ext = tm + 2 * HALO
    start = (pl.program_id(0) * tm) % seq
    hp, hn = hp_ref[...], hn_ref[...]
    hc_scr[0:HALO, :] = jnp.where(start == 0, jnp.zeros_like(hp), hp)
    hc_scr[HALO:HALO + tm, :] = h_ref[...]
    hc_scr[HALO + tm:rows_ext, :] = jnp.where(start + tm == seq, jnp.zeros_like(hn), hn)

    def chunk_cols(j):
        return pl.ds(pl.multiple_of(j * FF_CHUNK, FF_CHUNK), FF_CHUNK)

    def up(j, g_scr, v_scr):
        hc = hc_scr[...]
        g_scr[...] = _bdot(hc, wu_ref[:, chunk_cols(j)])
        v_scr[...] = _bdot(hc, wu_ref[:, chunk_cols(nchunk + j)])

    def conv(u_scr, j):
        u = u_scr[...]
        w = cw_ref[:, chunk_cols(j)]
        prev = pltpu.roll(u, 1, 0)[HALO:HALO + tm]
        nxt = pltpu.roll(u, rows_ext - 1, 0)[HALO:HALO + tm]
        return prev * w[0:1] + u[HALO:HALO + tm] * w[1:2] + nxt * w[2:3] + cb_ref[:, chunk_cols(j)]

    def activate(j, g_scr, v_scr):
        gate = conv(g_scr, j)
        val = conv(v_scr, nchunk + j)
        return (gate * jax.nn.sigmoid(gate) * val).astype(BF16)

    def down(j, act):
        o_ref[:, :half] += _bdot(act, wd_ref[j, :, :half])
        o_ref[:, half:] += _bdot(act, wd_ref[j, :, half:])

    o_ref[...] = jnp.zeros_like(o_ref)
    up(0, ga_scr, va_scr)

    def pair(i, carry):
        j = 2 * i
        act = activate(j, ga_scr, va_scr)
        up(j + 1, gb_scr, vb_scr)
        down(j, act)
        act = activate(j + 1, gb_scr, vb_scr)
        up(j + 2, ga_scr, va_scr)
        down(j + 1, act)
        return carry

    lax.fori_loop(0, (nchunk - 1) // 2, pair, 0)
    down(nchunk - 1, activate(nchunk - 1, ga_scr, va_scr))
    o_ref[...] = nx_ref[...] + mod_ref[0, :, 5 * d:6 * d] * o_ref[...]


def _ffn(h2, nx, modl, wu, cw, cb, wd, *, tm, seq, mod_row):
    n, d = nx.shape
    assert seq % tm == 0 and (D_FF // FF_CHUNK) % 2 == 1
    nh = n // HALO
    per = tm // HALO
    rows_ext = tm + 2 * HALO
    row = lambda width: pl.BlockSpec((tm, width), lambda i: (i, 0))
    return pl.pallas_call(
        functools.partial(_ffn_kernel, tm=tm, seq=seq),
        scratch_shapes=[pltpu.VMEM((rows_ext, d), BF16)] + [pltpu.VMEM((rows_ext, FF_CHUNK), F32)] * 4,
        out_shape=jax.ShapeDtypeStruct((n, d), F32),
        grid=(n // tm,),
        in_specs=[row(d),
                  pl.BlockSpec((HALO, d), lambda i: (jnp.maximum(i * per - 1, 0), 0)),
                  pl.BlockSpec((HALO, d), lambda i: (jnp.minimum((i + 1) * per, nh - 1), 0)),
                  row(d), _mod_spec(mod_row),
                  _resident(wu.shape), _resident(cw.shape), _resident(cb.shape), _resident(wd.shape)],
        out_specs=row(d),
        compiler_params=_params(("parallel",)),
        name="conv_ffn",
    )(h2, h2, h2, nx, modl, wu, cw, cb, wd)


def _pad_to(a, shape):
    return jnp.pad(a, [(0, t - s) for s, t in zip(a.shape, shape)])


def kernel(x, c, ctx, c_ctx, w_mod, b_mod, norm1_g, w_in, hy_conv_w, hy_conv_b, hy_w1, hy_b1, hy_w2, hy_b2,
           hy_w3, hy_freq, hy_bias, sg_ln_g, sg_ln_b, sg_w, sg_b, q_norm_g, k_norm_g, lam_q1, lam_k1, lam_q2,
           lam_k2, subln_g, w_out, norm2_g, ffn_w_up, ffn_conv_w, ffn_conv_b, ffn_w_down):
    bsz, seq, d = x.shape
    lc = ctx.shape[1]
    depth = w_mod.shape[0]
    assert d == D_MODEL and bsz + 1 <= MOD_ROWS and (bsz * lc) % 512 == 0 and seq % 512 == 0

    cc = _pad_to(jnp.concatenate([c, c_ctx[None, :]], axis=0), (MOD_ROWS, d))
    mod = _modulation(cc, w_mod, b_mod)
    ctx_row = bsz

    w1p = _pad_to(hy_w1, (depth, LANES, LANES))
    b1p = _pad_to(hy_b1, (depth, LANES)).reshape(depth, 1, LANES)
    frp = _pad_to(hy_freq, (depth, LANES)).reshape(depth, 1, LANES)
    w2p = _pad_to(hy_w2, (depth, LANES, LANES))
    b2p = _pad_to(hy_b2, (depth, LANES)).reshape(depth, 1, LANES)
    w3p = _pad_to(hy_w3, (depth, LANES, HY_ORDER * 2 * D_HY))
    lb = seq // 2
    dft = {}
    for s in (lb, lc):
        cmat, smat, _, _ = _dft_tables(s)
        dft[s] = (jnp.asarray(cmat).astype(BF16), jnp.asarray(smat).astype(BF16))
    ktab = _hyena_filters_blocked(seq, lb, w1p, b1p, frp, w2p, b2p, w3p, *dft[lb])
    nctx = depth - 1
    ktab_c = _hyena_filters(lc, w1p[:nctx], b1p[:nctx], frp[:nctx], w2p[:nctx], b2p[:nctx], w3p[:nctx], *dft[lc])

    tm = 512
    nchunk = D_FF // FF_CHUNK
    xs = x.reshape(bsz * seq, d)
    cs = ctx.reshape(bsz * lc, d)
    lat_row = lambda i: (i * tm) // seq
    ctx_mod = lambda i: ctx_row

    for l in range(depth):
        last = l == depth - 1
        lam_init = 0.8 - 0.6 * math.exp(-0.3 * l)
        modl = mod[l].reshape(MOD_ROWS, 1, 6 * d)
        g1 = norm1_g[l].reshape(1, d)
        g2 = norm2_g[l].reshape(1, d)
        w_in_l = w_in[l]
        w_out_b = w_out[l].astype(BF16)
        wu = ffn_w_up[l].astype(BF16)
        wd = ffn_w_down[l].astype(BF16).reshape(nchunk, FF_CHUNK, d)
        fcw = ffn_conv_w[l]
        fcb = ffn_conv_b[l].reshape(1, 2 * D_FF)
        qn = jnp.tile(q_norm_g[l], LANES // QK_DIM).reshape(1, LANES)
        kn = jnp.tile(k_norm_g[l], LANES // QK_DIM).reshape(1, LANES)
        lng = sg_ln_g[l].reshape(1, D_SG)
        lnb = sg_ln_b[l].reshape(1, D_SG)
        wcat = sg_w[l].transpose(1, 0, 2).reshape(SG_CHUNK, SG_GROUPS * SG_CHUNK).astype(BF16)
        bstab = jnp.repeat(sg_b[l].T, D_SG // SG_GROUPS, axis=1)
        lamv = _pad_to(jnp.stack([lam_q1[l], lam_k1[l], lam_q2[l], lam_k2[l]]), (4, LANES))
        subg = subln_g[l].reshape(1, V_DIM)
        hcw = hy_conv_w[l]
        hcb = hy_conv_b[l].reshape(1, IN_A)
        sgu = (qn, kn, lng, lnb, wcat, bstab)

        pa, osg, q, k, v = _in_proj(xs, modl, g1, w_in_l, *sgu, seq=seq, tm=tm, use_rope=True, mod_row=lat_row)
        if last:
            w_kv_b = w_in_l[:, IN_A + IN_B + D_QK:].astype(BF16)
            kc, vc = _in_proj_kv(cs, modl, g1, w_kv_b, kn, tm=tm, mod_row=ctx_mod)
        else:
            pa_c, osg_c, qc, kc, vc = _in_proj(cs, modl, g1, w_in_l, *sgu, seq=lc, tm=tm, use_rope=False,
                                               mod_row=ctx_mod)
        b3 = lambda a, t: a.reshape(bsz, t, a.shape[-1])
        kc3, vc3 = b3(kc, lc), b3(vc, lc)
        oat = _attention(lamv, b3(q, seq), [(b3(k, seq), b3(v, seq), ATT_KEY_PIECES), (kc3, vc3, 1)], subg,
                         tq=256, lam_init=lam_init)
        ohy = _hyena_blocked(b3(pa, seq), hcw, hcb, *dft[lb], ktab, hy_bias[l], layer=l, lb=lb, rb=512)
        nx, h2 = _out_proj(ohy.reshape(-1, D_HY), osg, oat.reshape(-1, D_ATT), xs, modl, g2, w_out_b,
                           tm=tm, mod_row=lat_row)
        xs_next = _ffn(h2, nx, modl, wu, fcw, fcb, wd, tm=tm, seq=seq, mod_row=lat_row)

        if not last:
            oat_c = _attention(lamv, b3(qc, lc), [(kc3, vc3, 1)], subg, tq=lc, lam_init=lam_init)
            ohy_c = _hyena(b3(pa_c, lc), hcw, hcb, *dft[lc], ktab_c[l], hy_bias[l], rb=lc)
            nx_c, h2_c = _out_proj(ohy_c.reshape(-1, D_HY), osg_c, oat_c.reshape(-1, D_ATT), cs, modl, g2,
                                   w_out_b, tm=tm, mod_row=ctx_mod)
            cs = _ffn(h2_c, nx_c, modl, wu, fcw, fcb, wd, tm=min(tm, lc), seq=lc, mod_row=ctx_mod)
        xs = xs_next
    return xs.reshape(bsz, seq, d)
```

```python
import functools
import math

import jax
import jax.numpy as jnp
import numpy as np
from jax import lax
from jax.experimental import pallas as pl
from jax.experimental.pallas import tpu as pltpu

F32 = jnp.float32
BF16 = jnp.bfloat16

D_MODEL = 1024
DEPTH = 2
GRID_W = 64
EPS = 1e-6
D_HY = 256
HY_ORDER = 2
HY_BANDS = 16
HY_EMB = 1 + 2 * HY_BANDS
HY_FFN = 64
HY_DECAY_TARGET = 1e-2
HY_FAST_DECAY = 0.3
HY_SLOW_DECAY = 1.5
SG_GROUPS = 4
D_SG = 256
SG_CHUNK = 128
ATT_HEADS = 4
QK_DIM = 64
V_DIM = 128
D_ATT = ATT_HEADS * V_DIM
ROPE_PAIRS = QK_DIM // 4
ROPE_BASE = 10000.0
D_MIX = D_HY + D_SG + D_ATT
IN_A = 3 * D_HY
IN_B = 2 * D_SG
D_QK = 2 * ATT_HEADS * QK_DIM
IN_C = 2 * D_QK + D_ATT
D_IN = IN_A + IN_B + IN_C
D_FF = 2816
D_VAUG = 2 * D_ATT
LOG2E = 1.4426950408889634

LANES = 128
BF16_SUBLANES = 16
VMEM_LIMIT_BYTES = 56 * 1024 * 1024

MOD_ROWS = 16
FF_CHUNK = 256
ATT_KEY_PIECES = 1
HALO = BF16_SUBLANES
CONV_ROWS = 128


def _params(sem, vmem=VMEM_LIMIT_BYTES):
    return pltpu.CompilerParams(dimension_semantics=sem, vmem_limit_bytes=vmem)


def _resident(shape):
    nd = len(shape)
    return pl.BlockSpec(shape, lambda *_: (0,) * nd, pipeline_mode=pl.Buffered(1))


@functools.lru_cache(maxsize=None)
def _dft_tables(seq):
    n2 = 2 * seq
    f = np.arange(seq, dtype=np.int64)
    m = ((2 * f[:, None] + 1) * (2 * f[None, :] + 1)) % (4 * n2)
    ang = (2.0 * np.pi / (4 * n2)) * m.astype(np.float64)
    half = np.pi * (2 * f + 1) / (2.0 * n2)
    return (np.cos(ang).astype(np.float32), np.sin(ang).astype(np.float32),
            np.cos(half).astype(np.float32)[:, None], np.sin(half).astype(np.float32)[:, None])


@functools.lru_cache(maxsize=None)
def _filter_tables(seq):
    t = np.linspace(0.0, 1.0, seq, dtype=np.float32)[:, None]
    t_r = np.arange(seq, dtype=np.float32)[:, None]
    bands = np.linspace(1e-4, HY_BANDS - 1, HY_BANDS, dtype=np.float32)[None, :]
    w = (2.0 * math.pi * t_r / seq).astype(np.float32)
    z = np.concatenate([t, np.cos(bands * w), -np.sin(bands * w)], axis=-1).astype(np.float32)
    zp = np.zeros((seq, LANES), np.float32)
    zp[:, :HY_EMB] = z
    min_decay = math.log(HY_DECAY_TARGET) / HY_SLOW_DECAY
    max_decay = math.log(HY_DECAY_TARGET) / HY_FAST_DECAY
    deltas = np.abs(np.linspace(min_decay, max_decay, D_HY, dtype=np.float32))
    decay = np.exp(-t * deltas[None, :]).astype(np.float32)
    return zp, decay


@functools.lru_cache(maxsize=None)
def _filter_tables_reversed(seq, lb):
    zp, decay = _filter_tables(seq)
    idx = lb - np.arange(lb)
    return np.ascontiguousarray(zp[idx]), np.ascontiguousarray(decay[idx])


@functools.lru_cache(maxsize=None)
def _rope_tables(seq):
    pos = np.arange(seq)
    row = (pos // GRID_W).astype(np.float32)
    col = (pos % GRID_W).astype(np.float32)
    inv = (ROPE_BASE ** (-np.arange(ROPE_PAIRS, dtype=np.float32) / ROPE_PAIRS)).astype(np.float32)
    lane = np.arange(LANES)
    axis = (lane % QK_DIM) // (2 * ROPE_PAIRS)
    half = (lane % (2 * ROPE_PAIRS)) // ROPE_PAIRS
    pair = lane % ROPE_PAIRS
    p = np.where(axis[None, :] == 0, row[:, None], col[:, None]).astype(np.float32)
    ang = (p * inv[pair][None, :]).astype(np.float32)
    cos, sin = np.cos(ang).astype(np.float32), np.sin(ang).astype(np.float32)
    sin_lo = np.where(half[None, :] == 0, -sin, 0.0).astype(np.float32)
    sin_hi = np.where(half[None, :] == 1, sin, 0.0).astype(np.float32)
    return cos, sin_lo, sin_hi


def _group_sum(x):
    lane = lax.broadcasted_iota(jnp.int32, (1, LANES), 1)
    lo = lane < QK_DIM
    s_lo = jnp.sum(jnp.where(lo, x, 0.0), axis=-1, keepdims=True)
    s_hi = jnp.sum(jnp.where(lo, 0.0, x), axis=-1, keepdims=True)
    return jnp.where(lo, s_lo, s_hi)


def _rmsnorm_rows(x, g):
    return x * lax.rsqrt(jnp.mean(x * x, axis=-1, keepdims=True) + EPS) * g


def _bdot(a, b):
    return jnp.dot(a, b, preferred_element_type=F32)


def _split_bf16(x):
    hi = x.astype(BF16)
    return hi, (x - hi.astype(F32)).astype(BF16)


def _dot3(a, b):
    a_hi, a_lo = _split_bf16(a)
    b_hi, b_lo = _split_bf16(b)
    return _bdot(a_hi, b_hi) + (_bdot(a_lo, b_hi) + _bdot(a_hi, b_lo))


def _v_aug(v):
    ones = jnp.ones((v.shape[0], V_DIM), BF16)
    vb = v.astype(BF16)
    parts = []
    for h in range(ATT_HEADS):
        parts += [vb[:, h * V_DIM:(h + 1) * V_DIM], ones]
    return jnp.concatenate(parts, axis=-1)


def _mod_kernel(cc_ref, w_ref, b_ref, o_ref):
    cc = cc_ref[...]
    a = cc * jax.nn.sigmoid(cc)
    o_ref[0] = _dot3(a, w_ref[0]) + b_ref[0]


def _modulation(cc, w_mod, b_mod):
    depth, d, n = w_mod.shape
    tn = 768
    return pl.pallas_call(
        _mod_kernel,
        out_shape=jax.ShapeDtypeStruct((depth, MOD_ROWS, n), F32),
        grid=(depth, n // tn),
        in_specs=[pl.BlockSpec((MOD_ROWS, d), lambda l, j: (0, 0)),
                  pl.BlockSpec((1, d, tn), lambda l, j: (l, 0, j)),
                  pl.BlockSpec((1, 1, tn), lambda l, j: (l, 0, j))],
        out_specs=pl.BlockSpec((1, MOD_ROWS, tn), lambda l, j: (l, 0, j)),
        compiler_params=_params(("parallel", "parallel")),
        name="adaln_modulation",
    )(cc, w_mod, b_mod.reshape(depth, 1, n))


def _filter_kernel(z_ref, w1_ref, b1_ref, fr_ref, w2_ref, b2_ref, w3_ref, dec_ref, cm_ref, sm_ref,
                   ch_ref, sh_ref, o_ref, *, seq):
    fr = fr_ref[0]
    h = jnp.sin(fr * (_dot3(z_ref[...], w1_ref[0]) + b1_ref[0]))
    h = jnp.sin(fr * (_dot3(h, w2_ref[0]) + b2_ref[0]))
    h = _dot3(h, w3_ref[0])
    dec = dec_ref[...]
    hf = h[:, :D_HY] * dec
    row = lax.broadcasted_iota(jnp.int32, (seq, 1), 0)
    hb = jnp.where(row > 0, h[:, D_HY:] * dec, 0.0)
    gp = (hf + hb).astype(BF16)
    gm = (hf - hb).astype(BF16)
    cm, sm = cm_ref[...], sm_ref[...]
    ch, sh = ch_ref[...], sh_ref[...]
    scale = 1.0 / seq
    o_ref[0, 0] = (_bdot(cm, gp) * ch + _bdot(sm, gp) * sh) * scale
    o_ref[0, 1] = (_bdot(cm, gm) * sh - _bdot(sm, gm) * ch) * scale


def _hyena_filters(seq, w1p, b1p, frp, w2p, b2p, w3, cm, sm):
    layers = w1p.shape[0]
    zp, decay = _filter_tables(seq)
    _, _, ch, sh = _dft_tables(seq)
    out = pl.pallas_call(
        functools.partial(_filter_kernel, seq=seq),
        out_shape=jax.ShapeDtypeStruct((layers * HY_ORDER, 2, seq, D_HY), F32),
        grid=(layers, HY_ORDER),
        in_specs=[_resident((seq, LANES)),
                  pl.BlockSpec((1, LANES, LANES), lambda l, o: (l, 0, 0)),
                  pl.BlockSpec((1, 1, LANES), lambda l, o: (l, 0, 0)),
                  pl.BlockSpec((1, 1, LANES), lambda l, o: (l, 0, 0)),
                  pl.BlockSpec((1, LANES, LANES), lambda l, o: (l, 0, 0)),
                  pl.BlockSpec((1, 1, LANES), lambda l, o: (l, 0, 0)),
                  pl.BlockSpec((1, LANES, 2 * D_HY), lambda l, o: (l, 0, o)),
                  _resident((seq, D_HY)), _resident((seq, seq)), _resident((seq, seq)),
                  _resident((seq, 1)), _resident((seq, 1))],
        out_specs=pl.BlockSpec((1, 2, seq, D_HY), lambda l, o: (l * HY_ORDER + o, 0, 0, 0)),
        compiler_params=_params(("parallel", "parallel")),
        name=f"hyena_filter_spectra_{seq}",
    )(jnp.asarray(zp), w1p, b1p, frp, w2p, b2p, w3, jnp.asarray(decay), cm, sm, jnp.asarray(ch), jnp.asarray(sh))
    return out.reshape(layers, 2 * HY_ORDER, seq, D_HY)


def _filter_blocked_kernel(z_ref, zr_ref, w1_ref, b1_ref, fr_ref, w2_ref, b2_ref, w3_ref, dec_ref, decr_ref,
                           cm_ref, sm_ref, ch_ref, sh_ref, o_ref, hid_scr, hidr_scr, *, lb):
    fr = fr_ref[0]

    def hidden(z):
        h = jnp.sin(fr * (_dot3(z, w1_ref[0]) + b1_ref[0]))
        return jnp.sin(fr * (_dot3(h, w2_ref[0]) + b2_ref[0]))

    @pl.when(pl.program_id(1) == 0)
    def _():
        hid_scr[...] = hidden(z_ref[...])
        hidr_scr[...] = hidden(zr_ref[...])

    hall = _dot3(jnp.concatenate([hid_scr[...], hidr_scr[...]], axis=0), w3_ref[0])
    hn, hr = hall[:2 * lb], hall[2 * lb:]
    dec, decr = dec_ref[...], decr_ref[...]
    hf, hb = hn[:, :D_HY] * dec, hn[:, D_HY:] * dec
    hfr, hbr = hr[:, :D_HY] * decr, hr[:, D_HY:] * decr
    taps = ((hf[:lb], hb[:lb]),
            (hf[lb:], hfr),
            (hbr, hb[lb:]))
    row = lax.broadcasted_iota(jnp.int32, (lb, 1), 0)
    cm, sm = cm_ref[...], sm_ref[...]
    ch, sh = ch_ref[...], sh_ref[...]
    scale = 1.0 / lb
    for d, (fwd, bwd) in enumerate(taps):
        bwd = jnp.where(row > 0, bwd, 0.0)
        gp = (fwd + bwd).astype(BF16)
        gm = (fwd - bwd).astype(BF16)
        o_ref[0, d, 0] = (_bdot(cm, gp) * ch + _bdot(sm, gp) * sh) * scale
        o_ref[0, d, 1] = (_bdot(cm, gm) * sh - _bdot(sm, gm) * ch) * scale


def _hyena_filters_blocked(seq, lb, w1p, b1p, frp, w2p, b2p, w3, cm, sm):
    assert seq == 2 * lb
    layers = w1p.shape[0]
    zp, decay = _filter_tables(seq)
    zr, decr = _filter_tables_reversed(seq, lb)
    _, _, ch, sh = _dft_tables(lb)
    per_layer = lambda l, o: (l, 0, 0)
    out = pl.pallas_call(
        functools.partial(_filter_blocked_kernel, lb=lb),
        out_shape=jax.ShapeDtypeStruct((layers * HY_ORDER, 3, 2, lb, D_HY), F32),
        grid=(layers, HY_ORDER),
        in_specs=[_resident((seq, LANES)), _resident((lb, LANES)),
                  pl.BlockSpec((1, LANES, LANES), per_layer), pl.BlockSpec((1, 1, LANES), per_layer),
                  pl.BlockSpec((1, 1, LANES), per_layer), pl.BlockSpec((1, LANES, LANES), per_layer),
                  pl.BlockSpec((1, 1, LANES), per_layer),
                  pl.BlockSpec((1, LANES, 2 * D_HY), lambda l, o: (l, 0, o)),
                  _resident((seq, D_HY)), _resident((lb, D_HY)), _resident((lb, lb)), _resident((lb, lb)),
                  _resident((lb, 1)), _resident((lb, 1))],
        out_specs=pl.BlockSpec((1, 3, 2, lb, D_HY), lambda l, o: (l * HY_ORDER + o, 0, 0, 0, 0)),
        scratch_shapes=[pltpu.VMEM((seq, LANES), F32), pltpu.VMEM((lb, LANES), F32)],
        compiler_params=_params(("arbitrary", "arbitrary")),
        name=f"hyena_block_filter_spectra_{seq}",
    )(jnp.asarray(zp), jnp.asarray(zr), w1p, b1p, frp, w2p, b2p, w3, jnp.asarray(decay), jnp.asarray(decr),
      cm, sm, jnp.asarray(ch), jnp.asarray(sh))
    return out.reshape(layers, HY_ORDER, 3, 2, lb, D_HY)


def _prenorm(x_ref, mod_ref, g_ref, shift_col, scale_col):
    d = D_MODEL
    shift = mod_ref[0, :, shift_col * d:(shift_col + 1) * d]
    scale = mod_ref[0, :, scale_col * d:(scale_col + 1) * d]
    return (_rmsnorm_rows(x_ref[...], g_ref[...]) * (1.0 + scale) + shift).astype(BF16)


def _qk_head_blocks(p, gain, rope):
    out = []
    for h in range(ATT_HEADS):
        xb = p[:, h * LANES:(h + 1) * LANES]
        xb = xb * lax.rsqrt(_group_sum(xb * xb) * (1.0 / QK_DIM) + EPS) * gain
        if rope is not None:
            cos, sin_lo, sin_hi = rope
            xb = (xb * cos + pltpu.roll(xb, LANES - ROPE_PAIRS, 1) * sin_lo
                  + pltpu.roll(xb, ROPE_PAIRS, 1) * sin_hi)
        out.append(xb)
    return jnp.concatenate(out, axis=-1)


def _in_kernel(*refs, tm, use_rope):
    if use_rope:
        (x_ref, mod_ref, g_ref, w_ref, cos_ref, slo_ref, shi_ref, qn_ref, kn_ref, lng_ref, lnb_ref,
         wcat_ref, bst_ref, pa_ref, osg_ref, q_ref, k_ref, v_ref) = refs
        rope = (cos_ref[...], slo_ref[...], shi_ref[...])
    else:
        (x_ref, mod_ref, g_ref, w_ref, qn_ref, kn_ref, lng_ref, lnb_ref,
         wcat_ref, bst_ref, pa_ref, osg_ref, q_ref, k_ref, v_ref) = refs
        rope = None
    hb = _prenorm(x_ref, mod_ref, g_ref, 0, 1)

    c0 = IN_A + IN_B
    pb = _bdot(hb, w_ref[:, IN_A:c0])
    pq = _bdot(hb, w_ref[:, c0:c0 + D_QK])
    pk = _bdot(hb, w_ref[:, c0 + D_QK:c0 + 2 * D_QK])

    pb = jax.nn.gelu(pb)
    u = pb[:, :D_SG]
    vn = []
    for j in range(D_SG // LANES):
        xb = pb[:, D_SG + j * LANES:D_SG + (j + 1) * LANES]
        xc = xb - _group_sum(xb) * (1.0 / QK_DIM)
        vn.append(xc * lax.rsqrt(_group_sum(xc * xc) * (1.0 / QK_DIM) + EPS))
    vn = (jnp.concatenate(vn, axis=-1) * lng_ref[...] + lnb_ref[...]).astype(BF16)
    lane = lax.broadcasted_iota(jnp.int32, (1, D_SG), 1)
    grp = lane // (D_SG // SG_GROUPS)
    zero = jnp.zeros((SG_CHUNK, D_SG), BF16)
    for ci in range(tm // SG_CHUNK):
        rows = slice(ci * SG_CHUNK, (ci + 1) * SG_CHUNK)
        vc = vn[rows]
        stacked = jnp.concatenate([jnp.where(grp == g, vc, zero) for g in range(SG_GROUPS)], axis=0)
        s = _bdot(wcat_ref[...], stacked) + bst_ref[...]
        osg_ref[rows, :] = (u[rows] * s).astype(BF16)

    q = _qk_head_blocks(pq, qn_ref[...], rope)
    q_ref[...] = (q * (QK_DIM ** -0.5 * LOG2E)).astype(BF16)
    pa_ref[...] = _bdot(hb, w_ref[:, :IN_A])
    k_ref[...] = _qk_head_blocks(pk, kn_ref[...], rope).astype(BF16)
    v_ref[...] = _v_aug(_bdot(hb, w_ref[:, c0 + 2 * D_QK:]))


def _in_kv_kernel(x_ref, mod_ref, g_ref, w_ref, kn_ref, k_ref, v_ref):
    hb = _prenorm(x_ref, mod_ref, g_ref, 0, 1)
    k = _qk_head_blocks(_bdot(hb, w_ref[:, :D_QK]), kn_ref[...], None)
    k_ref[...] = k.astype(BF16)
    v_ref[...] = _v_aug(_bdot(hb, w_ref[:, D_QK:]))


def _mod_spec(mod_row):
    return pl.BlockSpec((1, 1, 6 * D_MODEL), lambda i: (mod_row(i), 0, 0))


def _in_proj(xs, modl, g1, w_in_b, qn, kn, lng, lnb, wcat, bstab, *, seq, tm, use_rope, mod_row):
    n, d = xs.shape
    row = lambda width: pl.BlockSpec((tm, width), lambda i: (i, 0))
    vec = lambda width: pl.BlockSpec((1, width), lambda i: (0, 0))
    in_specs = [row(d), _mod_spec(mod_row), vec(d), _resident((d, D_IN))]
    args = [xs, modl, g1, w_in_b]
    if use_rope:
        nblk = seq // tm
        in_specs += [pl.BlockSpec((tm, LANES), lambda i: (i % nblk, 0))] * 3
        args += [jnp.asarray(t) for t in _rope_tables(seq)]
    in_specs += [vec(LANES), vec(LANES), vec(D_SG), vec(D_SG), _resident((SG_CHUNK, SG_GROUPS * SG_CHUNK)),
                 _resident((SG_CHUNK, D_SG))]
    args += [qn, kn, lng, lnb, wcat, bstab]
    return pl.pallas_call(
        functools.partial(_in_kernel, tm=tm, use_rope=use_rope),
        out_shape=(jax.ShapeDtypeStruct((n, IN_A), F32), jax.ShapeDtypeStruct((n, D_SG), BF16),
                   jax.ShapeDtypeStruct((n, D_QK), BF16), jax.ShapeDtypeStruct((n, D_QK), BF16),
                   jax.ShapeDtypeStruct((n, D_VAUG), BF16)),
        grid=(n // tm,),
        in_specs=in_specs,
        out_specs=(row(IN_A), row(D_SG), row(D_QK), row(D_QK), row(D_VAUG)),
        compiler_params=_params(("parallel",)),
        name="in_proj_rope" if use_rope else "in_proj",
    )(*args)


def _in_proj_kv(xs, modl, g1, w_kv_b, kn, *, tm, mod_row):
    n, d = xs.shape
    row = lambda width: pl.BlockSpec((tm, width), lambda i: (i, 0))
    vec = lambda width: pl.BlockSpec((1, width), lambda i: (0, 0))
    return pl.pallas_call(
        _in_kv_kernel,
        out_shape=(jax.ShapeDtypeStruct((n, D_QK), BF16), jax.ShapeDtypeStruct((n, D_VAUG), BF16)),
        grid=(n // tm,),
        in_specs=[row(d), _mod_spec(mod_row), vec(d), _resident((d, D_QK + D_ATT)), vec(LANES)],
        out_specs=(row(D_QK), row(D_VAUG)),
        compiler_params=_params(("parallel",)),
        name="in_proj_kv",
    )(xs, modl, g1, w_kv_b, kn)


def _att_kernel(*refs, segments, lam_init, tq):
    lam_ref, q_ref = refs[0], refs[1]
    kv_refs = refs[2:2 + 2 * segments]
    g_ref, o_ref = refs[2 + 2 * segments], refs[3 + 2 * segments]
    lv = lam_ref[...]
    lam = (jnp.exp(jnp.sum(lv[0:1] * lv[1:2], axis=-1, keepdims=True))
           - jnp.exp(jnp.sum(lv[2:3] * lv[3:4], axis=-1, keepdims=True)) + lam_init)
    lo = lax.broadcasted_iota(jnp.int32, (1, LANES), 1) < QK_DIM
    nt = (((1,), (1,)), ((), ()))

    def head_scores(h):
        cols = slice(h * LANES, (h + 1) * LANES)
        qh = q_ref[0, :, cols]
        zero = jnp.zeros_like(qh)
        return [[lax.dot_general(qm, kv_refs[2 * i][0, :, cols], nt, preferred_element_type=F32)
                 for i in range(segments)]
                for qm in (jnp.where(lo, qh, zero), jnp.where(lo, zero, qh))]

    nxt = head_scores(0)
    for h in range(ATT_HEADS):
        cols = slice(h * LANES, (h + 1) * LANES)
        scores, nxt = nxt, (head_scores(h + 1) if h + 1 < ATT_HEADS else None)
        o = []
        for s in scores:
            m = jnp.max(s[0], axis=-1, keepdims=True)
            for si in s[1:]:
                m = jnp.maximum(m, jnp.max(si, axis=-1, keepdims=True))
            acc = None
            for i in range(segments):
                p = jnp.exp2((s[i] - m).astype(BF16))
                a_i = _bdot(p, kv_refs[2 * i + 1][0, :, 2 * h * V_DIM:2 * (h + 1) * V_DIM])
                acc = a_i if acc is None else acc + a_i
            o.append(acc[:, :V_DIM] / acc[:, V_DIM:])
        out = o[0] - lam * o[1]
        o_ref[0, :, cols] = (_rmsnorm_rows(out, g_ref[...]) * (1.0 - lam_init)).astype(BF16)


def _attention(lamv, q, kvs, subg, *, tq, lam_init):
    b, t, _ = q.shape
    in_specs = [pl.BlockSpec((4, LANES), lambda i, j: (0, 0)),
                pl.BlockSpec((1, tq, D_QK), lambda i, j: (i, j, 0))]
    args = [lamv, q]
    for k, v, pieces in kvs:
        tk = k.shape[1] // pieces
        for piece in range(pieces):
            in_specs += [pl.BlockSpec((1, tk, D_QK), lambda i, j, piece=piece: (i, piece, 0)),
                         pl.BlockSpec((1, tk, D_VAUG), lambda i, j, piece=piece: (i, piece, 0))]
            args += [k, v]
    in_specs.append(pl.BlockSpec((1, V_DIM), lambda i, j: (0, 0)))
    args.append(subg)
    nseg = (len(in_specs) - 3) // 2
    return pl.pallas_call(
        functools.partial(_att_kernel, segments=nseg, lam_init=lam_init, tq=tq),
        out_shape=jax.ShapeDtypeStruct((b, t, D_ATT), BF16),
        grid=(b, t // tq),
        in_specs=in_specs,
        out_specs=pl.BlockSpec((1, tq, D_ATT), lambda i, j: (i, j, 0)),
        compiler_params=_params(("parallel", "parallel")),
        name=f"diff_attention_{nseg}seg",
    )(*args)


def _hyena_kernel(pa_ref, cw_ref, cb_ref, cm_ref, sm_ref, kt_ref, bias_ref, o_ref,
                  p_scr, z_scr, zb_scr, pb_scr, qb_scr, *, seq, rb):
    row = lax.broadcasted_iota(jnp.int32, (seq, 1), 0)
    for j in range(IN_A // D_HY):
        cols = slice(j * D_HY, (j + 1) * D_HY)
        pa = pa_ref[0, :, cols]
        prev = jnp.where(row > 0, pltpu.roll(pa, 1, 0), 0.0)
        nxt = jnp.where(row < seq - 1, pltpu.roll(pa, seq - 1, 0), 0.0)
        p_scr[:, cols] = (prev * cw_ref[0:1, cols] + pa * cw_ref[1:2, cols] + nxt * cw_ref[2:3, cols]
                          + cb_ref[:, cols])

    z_scr[...] = p_scr[:, :D_HY]
    for o in range(HY_ORDER):
        zb_scr[...] = z_scr[...].astype(BF16)
        gate_cols = slice((o + 1) * D_HY, (o + 2) * D_HY)

        def spectrum(i, carry):
            rows = pl.ds(pl.multiple_of(i * rb, rb), rb)
            a = _bdot(cm_ref[rows, :], zb_scr[...])
            b = _bdot(sm_ref[rows, :], zb_scr[...])
            kre, kim = kt_ref[2 * o, rows, :], kt_ref[2 * o + 1, rows, :]
            pb_scr[rows, :] = (a * kre + b * kim).astype(BF16)
            qb_scr[rows, :] = (b * kre - a * kim).astype(BF16)
            return carry

        lax.fori_loop(0, seq // rb, spectrum, 0)

        def synth(i, carry):
            rows = pl.ds(pl.multiple_of(i * rb, rb), rb)
            y = _bdot(cm_ref[rows, :], pb_scr[...]) + _bdot(sm_ref[rows, :], qb_scr[...])
            z_scr[rows, :] = p_scr[rows, gate_cols] * (y + bias_ref[o:o + 1, :] * z_scr[rows, :])
            return carry

        lax.fori_loop(0, seq // rb, synth, 0)
    o_ref[0] = z_scr[...].astype(BF16)


def _hyena(pa, cw, cb, cm, sm, ktab, bias, *, rb):
    b, seq, _ = pa.shape
    return pl.pallas_call(
        functools.partial(_hyena_kernel, seq=seq, rb=rb),
        out_shape=jax.ShapeDtypeStruct((b, seq, D_HY), BF16),
        grid=(b,),
        in_specs=[pl.BlockSpec((1, seq, IN_A), lambda i: (i, 0, 0), pipeline_mode=pl.Buffered(1)),
                  _resident((3, IN_A)), _resident((1, IN_A)), _resident((seq, seq)), _resident((seq, seq)),
                  _resident((2 * HY_ORDER, seq, D_HY)), _resident((HY_ORDER, D_HY))],
        out_specs=pl.BlockSpec((1, seq, D_HY), lambda i: (i, 0, 0)),
        scratch_shapes=[pltpu.VMEM((seq, IN_A), F32), pltpu.VMEM((seq, D_HY), F32),
                        pltpu.VMEM((seq, D_HY), BF16), pltpu.VMEM((seq, D_HY), BF16),
                        pltpu.VMEM((seq, D_HY), BF16)],
        compiler_params=_params(("parallel",)),
        name=f"hyena_mixer_{seq}",
    )(pa, cw, cb, cm, sm, ktab, bias)


def _hyena_blocked_kernel(pa_ref, cw_ref, cb_ref, cm_ref, sm_ref, kt_ref, bias_ref, o_ref,
                          p_scr, z_scr, zb_scr, pb_scr, qb_scr, *, seq, lb, rb):
    row = lax.broadcasted_iota(jnp.int32, (seq, 1), 0)
    for j in range(IN_A // D_HY):
        cols = slice(j * D_HY, (j + 1) * D_HY)
        pa = pa_ref[0, :, cols]
        prev = jnp.where(row > 0, pltpu.roll(pa, 1, 0), 0.0)
        nxt = jnp.where(row < seq - 1, pltpu.roll(pa, seq - 1, 0), 0.0)
        p_scr[:, cols] = (prev * cw_ref[0:1, cols] + pa * cw_ref[1:2, cols] + nxt * cw_ref[2:3, cols]
                          + cb_ref[:, cols])

    per_block = lb // rb
    z_scr[...] = p_scr[:, :D_HY]
    for o in range(HY_ORDER):
        zb_scr[...] = z_scr[...].astype(BF16)
        gate_cols = slice((o + 1) * D_HY, (o + 2) * D_HY)

        def spectrum(i, carry):
            rows = pl.ds(pl.multiple_of(i * rb, rb), rb)
            cmr, smr = cm_ref[rows, :], sm_ref[rows, :]
            a0, b0 = _bdot(cmr, zb_scr[0:lb, :]), _bdot(smr, zb_scr[0:lb, :])
            a1, b1 = _bdot(cmr, zb_scr[lb:seq, :]), _bdot(smr, zb_scr[lb:seq, :])
            k0r, k0i = kt_ref[o, 0, 0, rows, :], kt_ref[o, 0, 1, rows, :]
            kpr, kpi = kt_ref[o, 1, 0, rows, :], kt_ref[o, 1, 1, rows, :]
            kmr, kmi = kt_ref[o, 2, 0, rows, :], kt_ref[o, 2, 1, rows, :]
            pb_scr[0, rows, :] = (a0 * k0r + b0 * k0i + a1 * kmr + b1 * kmi).astype(BF16)
            qb_scr[0, rows, :] = (b0 * k0r - a0 * k0i + b1 * kmr - a1 * kmi).astype(BF16)
            pb_scr[1, rows, :] = (a1 * k0r + b1 * k0i + a0 * kpr + b0 * kpi).astype(BF16)
            qb_scr[1, rows, :] = (b1 * k0r - a1 * k0i + b0 * kpr - a0 * kpi).astype(BF16)
            return carry

        lax.fori_loop(0, per_block, spectrum, 0)

        def synth(i, carry):
            blk = i // per_block
            rows_in = pl.ds(pl.multiple_of((i % per_block) * rb, rb), rb)
            rows_out = pl.ds(pl.multiple_of(i * rb, rb), rb)
            y = _bdot(cm_ref[rows_in, :], pb_scr[blk]) + _bdot(sm_ref[rows_in, :], qb_scr[blk])
            z_scr[rows_out, :] = p_scr[rows_out, gate_cols] * (y + bias_ref[o:o + 1, :] * z_scr[rows_out, :])
            return carry

        lax.fori_loop(0, seq // rb, synth, 0)
    o_ref[0] = z_scr[...].astype(BF16)


def _hyena_blocked(pa, cw, cb, cm, sm, ktab, bias, *, layer, lb, rb):
    b, seq, _ = pa.shape
    ktab_spec = pl.BlockSpec((None, HY_ORDER, 3, 2, lb, D_HY), lambda i: (layer, 0, 0, 0, 0, 0),
                             pipeline_mode=pl.Buffered(1))
    return pl.pallas_call(
        functools.partial(_hyena_blocked_kernel, seq=seq, lb=lb, rb=rb),
        out_shape=jax.ShapeDtypeStruct((b, seq, D_HY), BF16),
        grid=(b,),
        in_specs=[pl.BlockSpec((1, seq, IN_A), lambda i: (i, 0, 0), pipeline_mode=pl.Buffered(1)),
                  _resident((3, IN_A)), _resident((1, IN_A)), _resident((lb, lb)), _resident((lb, lb)),
                  ktab_spec, _resident((HY_ORDER, D_HY))],
        out_specs=pl.BlockSpec((1, seq, D_HY), lambda i: (i, 0, 0)),
        scratch_shapes=[pltpu.VMEM((seq, IN_A), F32), pltpu.VMEM((seq, D_HY), F32),
                        pltpu.VMEM((seq, D_HY), BF16), pltpu.VMEM((2, lb, D_HY), BF16),
                        pltpu.VMEM((2, lb, D_HY), BF16)],
        compiler_params=_params(("parallel",)),
        name=f"hyena_block_mixer_{seq}",
    )(pa, cw, cb, cm, sm, ktab, bias)


def _out_kernel(ohy_ref, osg_ref, oat_ref, x_ref, mod_ref, g2_ref, w_ref, nx_ref, h2_ref):
    d = D_MODEL
    mix = (_bdot(ohy_ref[...], w_ref[:D_HY]) + _bdot(osg_ref[...], w_ref[D_HY:D_HY + D_SG])
           + _bdot(oat_ref[...], w_ref[D_HY + D_SG:]))
    nx = x_ref[...] + mod_ref[0, :, 2 * d:3 * d] * mix
    nx_ref[...] = nx
    h2 = _rmsnorm_rows(nx, g2_ref[...]) * (1.0 + mod_ref[0, :, 4 * d:5 * d]) + mod_ref[0, :, 3 * d:4 * d]
    h2_ref[...] = h2.astype(BF16)


def _out_proj(ohy, osg, oat, xs, modl, g2, w_out_b, *, tm, mod_row):
    n, d = xs.shape
    row = lambda width: pl.BlockSpec((tm, width), lambda i: (i, 0))
    return pl.pallas_call(
        _out_kernel,
        out_shape=(jax.ShapeDtypeStruct((n, d), F32), jax.ShapeDtypeStruct((n, d), BF16)),
        grid=(n // tm,),
        in_specs=[row(D_HY), row(D_SG), row(D_ATT), row(d), _mod_spec(mod_row),
                  pl.BlockSpec((1, d), lambda i: (0, 0)), _resident((D_MIX, d))],
        out_specs=(row(d), row(d)),
        compiler_params=_params(("parallel",)),
        name="out_proj",
    )(ohy, osg, oat, xs, modl, g2, w_out_b)


def _ffn_kernel(h_ref, hp_ref, hn_ref, nx_ref, mod_ref, wu_ref, cw_ref, cb_ref, wd_ref, o_ref,
                hc_scr, ga_scr, va_scr, gb_scr, vb_scr, *, tm, seq):
    d = D_MODEL
    half = d // 2
    nchunk = D_FF // FF_CHUNK
    rows_ext = tm + 2 * HALO
    start = (pl.program_id(0) * tm) % seq
    hp, hn = hp_ref[...], hn_ref[...]
    hc_scr[0:HALO, :] = jnp.where(start == 0, jnp.zeros_like(hp), hp)
    hc_scr[HALO:HALO + tm, :] = h_ref[...]
    hc_scr[HALO + tm:rows_ext, :] = jnp.where(start + tm == seq, jnp.zeros_like(hn), hn)

    def chunk_cols(j):
        return pl.ds(pl.multiple_of(j * FF_CHUNK, FF_CHUNK), FF_CHUNK)

    def up(j, g_scr, v_scr):
        hc = hc_scr[...]
        g_scr[...] = _bdot(hc, wu_ref[:, chunk_cols(j)])
        v_scr[...] = _bdot(hc, wu_ref[:, chunk_cols(nchunk + j)])

    def conv(u_scr, j, r):
        u = u_scr[HALO - 8 + r * CONV_ROWS:HALO + 8 + (r + 1) * CONV_ROWS, :]
        w = cw_ref[:, chunk_cols(j)]
        prev = pltpu.roll(u, 1, 0)[8:8 + CONV_ROWS]
        nxt = pltpu.roll(u, CONV_ROWS + 15, 0)[8:8 + CONV_ROWS]
        return prev * w[0:1] + u[8:8 + CONV_ROWS] * w[1:2] + nxt * w[2:3] + cb_ref[:, chunk_cols(j)]

    def activate(j, g_scr, v_scr):
        parts = []
        for r in range(tm // CONV_ROWS):
            gate = conv(g_scr, j, r)
            val = conv(v_scr, nchunk + j, r)
            parts.append((gate * jax.nn.sigmoid(gate) * val).astype(BF16))
        return jnp.concatenate(parts, axis=0)

    def down(j, act):
        o_ref[:, :half] += _bdot(act, wd_ref[j, :, :half])
        o_ref[:, half:] += _bdot(act, wd_ref[j, :, half:])

    o_ref[...] = jnp.zeros_like(o_ref)
    up(0, ga_scr, va_scr)

    def step(j, src, dst):
        hc = hc_scr[...]
        dst[0][...] = _bdot(hc, wu_ref[:, chunk_cols(j + 1)])
        down(j, activate(j, *src))
        dst[1][...] = _bdot(hc, wu_ref[:, chunk_cols(nchunk + j + 1)])

    def pair(i, carry):
        j = 2 * i
        step(j, (ga_scr, va_scr), (gb_scr, vb_scr))
        step(j + 1, (gb_scr, vb_scr), (ga_scr, va_scr))
        return carry

    lax.fori_loop(0, (nchunk - 1) // 2, pair, 0)
    down(nchunk - 1, activate(nchunk - 1, ga_scr, va_scr))
    o_ref[...] = nx_ref[...] + mod_ref[0, :, 5 * d:6 * d] * o_ref[...]


def _ffn(h2, nx, modl, wu, cw, cb, wd, *, tm, seq, mod_row):
    n, d = nx.shape
    assert seq % tm == 0 and (D_FF // FF_CHUNK) % 2 == 1
    nh = n // HALO
    per = tm // HALO
    rows_ext = tm + 2 * HALO
    row = lambda width: pl.BlockSpec((tm, width), lambda i: (i, 0))
    return pl.pallas_call(
        functools.partial(_ffn_kernel, tm=tm, seq=seq),
        scratch_shapes=[pltpu.VMEM((rows_ext, d), BF16)] + [pltpu.VMEM((rows_ext, FF_CHUNK), F32)] * 4,
        out_shape=jax.ShapeDtypeStruct((n, d), F32),
        grid=(n // tm,),
        in_specs=[row(d),
                  pl.BlockSpec((HALO, d), lambda i: (jnp.maximum(i * per - 1, 0), 0)),
                  pl.BlockSpec((HALO, d), lambda i: (jnp.minimum((i + 1) * per, nh - 1), 0)),
                  row(d), _mod_spec(mod_row),
                  _resident(wu.shape), _resident(cw.shape), _resident(cb.shape), _resident(wd.shape)],
        out_specs=row(d),
        compiler_params=_params(("parallel",)),
        name="conv_ffn",
    )(h2, h2, h2, nx, modl, wu, cw, cb, wd)


def _pad_to(a, shape):
    return jnp.pad(a, [(0, t - s) for s, t in zip(a.shape, shape)])


def kernel(x, c, ctx, c_ctx, w_mod, b_mod, norm1_g, w_in, hy_conv_w, hy_conv_b, hy_w1, hy_b1, hy_w2, hy_b2,
           hy_w3, hy_freq, hy_bias, sg_ln_g, sg_ln_b, sg_w, sg_b, q_norm_g, k_norm_g, lam_q1, lam_k1, lam_q2,
           lam_k2, subln_g, w_out, norm2_g, ffn_w_up, ffn_conv_w, ffn_conv_b, ffn_w_down):
    bsz, seq, d = x.shape
    lc = ctx.shape[1]
    depth = w_mod.shape[0]
    assert d == D_MODEL and bsz + 1 <= MOD_ROWS and (bsz * lc) % 512 == 0 and seq % 512 == 0

    cc = _pad_to(jnp.concatenate([c, c_ctx[None, :]], axis=0), (MOD_ROWS, d))
    mod = _modulation(cc, w_mod, b_mod)
    ctx_row = bsz

    w1p = _pad_to(hy_w1, (depth, LANES, LANES))
    b1p = _pad_to(hy_b1, (depth, LANES)).reshape(depth, 1, LANES)
    frp = _pad_to(hy_freq, (depth, LANES)).reshape(depth, 1, LANES)
    w2p = _pad_to(hy_w2, (depth, LANES, LANES))
    b2p = _pad_to(hy_b2, (depth, LANES)).reshape(depth, 1, LANES)
    w3p = _pad_to(hy_w3, (depth, LANES, HY_ORDER * 2 * D_HY))
    lb = seq // 2
    dft = {}
    for s in (lb, lc):
        cmat, smat, _, _ = _dft_tables(s)
        dft[s] = (jnp.asarray(cmat).astype(BF16), jnp.asarray(smat).astype(BF16))
    ktab = _hyena_filters_blocked(seq, lb, w1p, b1p, frp, w2p, b2p, w3p, *dft[lb])
    nctx = depth - 1
    ktab_c = _hyena_filters(lc, w1p[:nctx], b1p[:nctx], frp[:nctx], w2p[:nctx], b2p[:nctx], w3p[:nctx], *dft[lc])

    tm = 512
    nchunk = D_FF // FF_CHUNK
    xs = x.reshape(bsz * seq, d)
    cs = ctx.reshape(bsz * lc, d)
    lat_row = lambda i: (i * tm) // seq
    ctx_mod = lambda i: ctx_row

    for l in range(depth):
        last = l == depth - 1
        lam_init = 0.8 - 0.6 * math.exp(-0.3 * l)
        modl = mod[l].reshape(MOD_ROWS, 1, 6 * d)
        g1 = norm1_g[l].reshape(1, d)
        g2 = norm2_g[l].reshape(1, d)
        w_in_b = w_in[l].astype(BF16)
        w_out_b = w_out[l].astype(BF16)
        wu = ffn_w_up[l].astype(BF16)
        wd = ffn_w_down[l].astype(BF16).reshape(nchunk, FF_CHUNK, d)
        fcw = ffn_conv_w[l]
        fcb = ffn_conv_b[l].reshape(1, 2 * D_FF)
        qn = jnp.tile(q_norm_g[l], LANES // QK_DIM).reshape(1, LANES)
        kn = jnp.tile(k_norm_g[l], LANES // QK_DIM).reshape(1, LANES)
        lng = sg_ln_g[l].reshape(1, D_SG)
        lnb = sg_ln_b[l].reshape(1, D_SG)
        wcat = sg_w[l].transpose(1, 0, 2).reshape(SG_CHUNK, SG_GROUPS * SG_CHUNK).astype(BF16)
        bstab = jnp.repeat(sg_b[l].T, D_SG // SG_GROUPS, axis=1)
        lamv = _pad_to(jnp.stack([lam_q1[l], lam_k1[l], lam_q2[l], lam_k2[l]]), (4, LANES))
        subg = subln_g[l].reshape(1, V_DIM)
        hcw = hy_conv_w[l]
        hcb = hy_conv_b[l].reshape(1, IN_A)
        sgu = (qn, kn, lng, lnb, wcat, bstab)

        pa, osg, q, k, v = _in_proj(xs, modl, g1, w_in_b, *sgu, seq=seq, tm=tm, use_rope=True, mod_row=lat_row)
        if last:
            kc, vc = _in_proj_kv(cs, modl, g1, w_in_b[:, IN_A + IN_B + D_QK:], kn, tm=tm, mod_row=ctx_mod)
        else:
            pa_c, osg_c, qc, kc, vc = _in_proj(cs, modl, g1, w_in_b, *sgu, seq=lc, tm=tm, use_rope=False,
                                               mod_row=ctx_mod)
        b3 = lambda a, t: a.reshape(bsz, t, a.shape[-1])
        kc3, vc3 = b3(kc, lc), b3(vc, lc)
        oat = _attention(lamv, b3(q, seq), [(b3(k, seq), b3(v, seq), ATT_KEY_PIECES), (kc3, vc3, 1)], subg,
                         tq=256, lam_init=lam_init)
        ohy = _hyena_blocked(b3(pa, seq), hcw, hcb, *dft[lb], ktab, hy_bias[l], layer=l, lb=lb, rb=512)
        nx, h2 = _out_proj(ohy.reshape(-1, D_HY), osg, oat.reshape(-1, D_ATT), xs, modl, g2, w_out_b,
                           tm=tm, mod_row=lat_row)
        xs_next = _ffn(h2, nx, modl, wu, fcw, fcb, wd, tm=tm, seq=seq, mod_row=lat_row)

        if not last:
            oat_c = _attention(lamv, b3(qc, lc), [(kc3, vc3, 1)], subg, tq=lc, lam_init=lam_init)
            ohy_c = _hyena(b3(pa_c, lc), hcw, hcb, *dft[lc], ktab_c[l], hy_bias[l], rb=lc)
            nx_c, h2_c = _out_proj(ohy_c.reshape(-1, D_HY), osg_c, oat_c.reshape(-1, D_ATT), cs, modl, g2,
                                   w_out_b, tm=tm, mod_row=ctx_mod)
            cs = _ffn(h2_c, nx_c, modl, wu, fcw, fcb, wd, tm=min(tm, lc), seq=lc, mod_row=ctx_mod)
        xs = xs_next
    return xs.reshape(bsz, seq, d)
```
